```python
import math
import jax, jax.numpy as jnp
from jax import lax
import numpy as np

D_MODEL = 1024
BATCH = 4
SEQ = 8192
DEPTH = 1
DEC_BATCH = 16
DEC_SEQ = 64
PAST_LEN = 4096

CHUNK = 64
QBLOCK = 128
HEAD_DIM = 64
SB_HEADS = 8
DIFF_HEADS = 4
SB_WIDTH = SB_HEADS * HEAD_DIM
DIFF_WIDTH = DIFF_HEADS * 2 * HEAD_DIM
MIX_WIDTH = SB_WIDTH + DIFF_WIDTH
IN_WIDTH = 3 * MIX_WIDTH
IN_SPLITS = (SB_WIDTH, 2 * SB_WIDTH, 3 * SB_WIDTH, 3 * SB_WIDTH + DIFF_WIDTH, 3 * SB_WIDTH + 2 * DIFF_WIDTH)
D_FF = 2816
ROPE_THETA = 10000.0
LN_EPS = 1e-5
LAMBDA_STD = 0.1
DEEPNORM_ALPHA = (2 * DEPTH) ** 0.25
DEEPNORM_BETA = (8 * DEPTH) ** -0.25

kernel_name = "stickbreak_diffattn_macaron_streaming_step"


def lambda_init_for(layer):
    return 0.8 - 0.6 * math.exp(-0.3 * layer)


def layer_norm(x, g, b):
    xf = x.astype(jnp.float32)
    mu = jnp.mean(xf, axis=-1, keepdims=True)
    var = jnp.mean(jnp.square(xf - mu), axis=-1, keepdims=True)
    return ((xf - mu) * lax.rsqrt(var + LN_EPS)).astype(x.dtype) * g + b


def rms_norm(x, g):
    xf = x.astype(jnp.float32)
    return (xf * lax.rsqrt(jnp.mean(xf * xf, axis=-1, keepdims=True) + LN_EPS)).astype(x.dtype) * g


def rope(x, pos):
    half = HEAD_DIM // 2
    inv_freq = ROPE_THETA ** (-jnp.arange(half, dtype=jnp.float32) / half)
    ang = pos.astype(jnp.float32)[:, None] * inv_freq[None, :]
    bshape = (pos.shape[0],) + (1,) * (x.ndim - 3) + (half,)
    cos = jnp.cos(ang).reshape(bshape)
    sin = jnp.sin(ang).reshape(bshape)
    xf = x.astype(jnp.float32)
    x1, x2 = xf[..., :half], xf[..., half:]
    return jnp.concatenate([x1 * cos - x2 * sin, x2 * cos + x1 * sin], axis=-1).astype(x.dtype)


def swiglu(x, wg, wu, wd):
    gate = jnp.einsum('btd,df->btf', x, wg)
    up = jnp.einsum('btd,df->btf', x, wu)
    return jnp.einsum('btf,fd->btd', jax.nn.silu(gate) * up, wd)


def query_blocks(q):
    Tq = q.shape[1]
    qb = QBLOCK if Tq % QBLOCK == 0 else Tq
    nb = Tq // qb
    qs = q.reshape((q.shape[0], nb, qb) + q.shape[2:])
    return jnp.moveaxis(qs, 1, 0), qb, nb


def stick_breaking_attention(q, k, v, pos0):
    B, Tq, H, d = q.shape
    Tk = k.shape[1]
    scale = d ** -0.5
    k_pos = jnp.arange(Tk)
    qs, qb, nb = query_blocks(q)

    def block(args):
        qblk, i = args
        q_pos = pos0 + i * qb + jnp.arange(qb)
        z = jnp.einsum('bqhd,bkhd->bhqk', qblk, k, preferred_element_type=jnp.float32) * scale
        visible = k_pos[None, :] < q_pos[:, None]
        log_1m_beta = jnp.where(visible, jax.nn.log_sigmoid(-z), 0.0)
        stick = lax.cumsum(log_1m_beta, axis=3, reverse=True) - log_1m_beta
        w = jnp.where(visible, jnp.exp(jax.nn.log_sigmoid(z) + stick), 0.0)
        return jnp.einsum('bhqk,bkhd->bqhd', w.astype(v.dtype), v)

    out = lax.map(block, (qs, jnp.arange(nb)))
    return jnp.moveaxis(out, 0, 1).reshape(B, Tq, H, d)


def differential_attention(q, k, v, pos0, lam):
    B, Tq, H, _, d = q.shape
    Tk = k.shape[1]
    scale = d ** -0.5
    k_chunk = jnp.arange(Tk) // CHUNK
    qs, qb, nb = query_blocks(q)

    def block(args):
        qblk, i = args
        q_chunk = (pos0 + i * qb + jnp.arange(qb)) // CHUNK
        s = jnp.einsum('bqhcd,bkhcd->bhcqk', qblk, k, preferred_element_type=jnp.float32) * scale
        visible = k_chunk[None, :] <= q_chunk[:, None]
        p = jax.nn.softmax(jnp.where(visible, s, -jnp.inf), axis=-1)
        pd = p[:, :, 0] - lam * p[:, :, 1]
        return jnp.einsum('bhqk,bkhe->bqhe', pd.astype(v.dtype), v)

    out = lax.map(block, (qs, jnp.arange(nb)))
    return jnp.moveaxis(out, 0, 1).reshape(B, Tq, H, 2 * d)


def token_mixing(h, c_sb_k, c_sb_v, c_diff_k, c_diff_v, pos0, p, lambda_init):
    B, T, _ = h.shape
    proj = jnp.einsum('btd,de->bte', h, p['w_in'])
    sb_q, sb_k, sb_v, df_q, df_k, df_v = jnp.split(proj, IN_SPLITS, axis=-1)
    pos = pos0 + jnp.arange(T)
    sb_q = sb_q.reshape(B, T, SB_HEADS, HEAD_DIM)
    sb_k = sb_k.reshape(B, T, SB_HEADS, HEAD_DIM)
    sb_v = sb_v.reshape(B, T, SB_HEADS, HEAD_DIM)
    df_q = rope(df_q.reshape(B, T, DIFF_HEADS, 2, HEAD_DIM), pos)
    df_k = rope(df_k.reshape(B, T, DIFF_HEADS, 2, HEAD_DIM), pos)
    df_v = df_v.reshape(B, T, DIFF_HEADS, 2 * HEAD_DIM)

    if c_sb_k is None:
        sb_k_all, sb_v_all, df_k_all, df_v_all = sb_k, sb_v, df_k, df_v
    else:
        sb_k_all = jnp.concatenate([c_sb_k, sb_k], axis=1)
        sb_v_all = jnp.concatenate([c_sb_v, sb_v], axis=1)
        df_k_all = jnp.concatenate([c_diff_k, df_k], axis=1)
        df_v_all = jnp.concatenate([c_diff_v, df_v], axis=1)

    o_sb = stick_breaking_attention(sb_q, sb_k_all, sb_v_all, pos0).reshape(B, T, SB_WIDTH)

    lam = (jnp.exp(jnp.sum(p['lambda_q1'].astype(jnp.float32) * p['lambda_k1'].astype(jnp.float32)))
           - jnp.exp(jnp.sum(p['lambda_q2'].astype(jnp.float32) * p['lambda_k2'].astype(jnp.float32)))
           + lambda_init)
    o_df = differential_attention(df_q, df_k_all, df_v_all, pos0, lam)
    o_df = (rms_norm(o_df, p['subln_g']) * (1.0 - lambda_init)).reshape(B, T, DIFF_WIDTH)

    out = jnp.einsum('bte,ed->btd', jnp.concatenate([o_sb, o_df], axis=-1), p['w_o'])
    return out, (sb_k, sb_v, df_k, df_v)


def encoder_layer(x, c_sb_k, c_sb_v, c_diff_k, c_diff_v, pos0, p, lambda_init):
    a = DEEPNORM_ALPHA
    x = layer_norm(a * x + 0.5 * swiglu(x, p['ffn1_wg'], p['ffn1_wu'], p['ffn1_wd']), p['ln1_g'], p['ln1_b'])
    mix, rows = token_mixing(x, c_sb_k, c_sb_v, c_diff_k, c_diff_v, pos0, p, lambda_init)
    x = layer_norm(a * x + mix, p['ln2_g'], p['ln2_b'])
    x = layer_norm(a * x + 0.5 * swiglu(x, p['ffn2_wg'], p['ffn2_wu'], p['ffn2_wd']), p['ln3_g'], p['ln3_b'])
    return x, rows


def setup_inputs(seed: int = 0) -> dict:
    key = jax.random.key(seed)
    ks = jax.random.split(key, 40)
    L = DEPTH

    def nrm(k, shape, scale):
        return jax.random.normal(k, shape, jnp.float32) * scale

    fin = D_MODEL ** -0.5
    w_in = jnp.concatenate([
        nrm(ks[6], (L, D_MODEL, SB_WIDTH), fin),
        nrm(ks[7], (L, D_MODEL, SB_WIDTH), fin),
        nrm(ks[8], (L, D_MODEL, SB_WIDTH), fin * DEEPNORM_BETA),
        nrm(ks[9], (L, D_MODEL, DIFF_WIDTH), fin),
        nrm(ks[10], (L, D_MODEL, DIFF_WIDTH), fin),
        nrm(ks[11], (L, D_MODEL, DIFF_WIDTH), fin * DEEPNORM_BETA),
    ], axis=-1)
    return {
        "x_prompt": nrm(ks[0], (BATCH, SEQ, D_MODEL), 1.0),
        "x_sample": nrm(ks[1], (DEC_BATCH, DEC_SEQ, D_MODEL), 1.0),
        "cache_sb_k": nrm(ks[2], (L, DEC_BATCH, PAST_LEN, SB_HEADS, HEAD_DIM), 1.0),
        "cache_sb_v": nrm(ks[3], (L, DEC_BATCH, PAST_LEN, SB_HEADS, HEAD_DIM), 1.0),
        "cache_diff_k": nrm(ks[4], (L, DEC_BATCH, PAST_LEN, DIFF_HEADS, 2, HEAD_DIM), 1.0),
        "cache_diff_v": nrm(ks[5], (L, DEC_BATCH, PAST_LEN, DIFF_HEADS, 2 * HEAD_DIM), 1.0),
        "ln1_g": 1.0 + nrm(ks[12], (L, D_MODEL), 0.02),
        "ln1_b": nrm(ks[13], (L, D_MODEL), 0.02),
        "ffn1_wg": nrm(ks[14], (L, D_MODEL, D_FF), fin * DEEPNORM_BETA),
        "ffn1_wu": nrm(ks[15], (L, D_MODEL, D_FF), fin * DEEPNORM_BETA),
        "ffn1_wd": nrm(ks[16], (L, D_FF, D_MODEL), D_FF ** -0.5 * DEEPNORM_BETA),
        "w_in": w_in,
        "lambda_q1": nrm(ks[17], (L, HEAD_DIM), LAMBDA_STD),
        "lambda_k1": nrm(ks[18], (L, HEAD_DIM), LAMBDA_STD),
        "lambda_q2": nrm(ks[19], (L, HEAD_DIM), LAMBDA_STD),
        "lambda_k2": nrm(ks[20], (L, HEAD_DIM), LAMBDA_STD),
        "subln_g": 1.0 + nrm(ks[21], (L, 2 * HEAD_DIM), 0.02),
        "w_o": nrm(ks[22], (L, MIX_WIDTH, D_MODEL), MIX_WIDTH ** -0.5 * DEEPNORM_BETA),
        "ln2_g": 1.0 + nrm(ks[23], (L, D_MODEL), 0.02),
        "ln2_b": nrm(ks[24], (L, D_MODEL), 0.02),
        "ffn2_wg": nrm(ks[25], (L, D_MODEL, D_FF), fin * DEEPNORM_BETA),
        "ffn2_wu": nrm(ks[26], (L, D_MODEL, D_FF), fin * DEEPNORM_BETA),
        "ffn2_wd": nrm(ks[27], (L, D_FF, D_MODEL), D_FF ** -0.5 * DEEPNORM_BETA),
        "ln3_g": 1.0 + nrm(ks[28], (L, D_MODEL), 0.02),
        "ln3_b": nrm(ks[29], (L, D_MODEL), 0.02),
    }


def reference(x_prompt, x_sample, cache_sb_k, cache_sb_v, cache_diff_k, cache_diff_v,
              ln1_g, ln1_b, ffn1_wg, ffn1_wu, ffn1_wd, w_in,
              lambda_q1, lambda_k1, lambda_q2, lambda_k2, subln_g, w_o,
              ln2_g, ln2_b, ffn2_wg, ffn2_wu, ffn2_wd, ln3_g, ln3_b):
    yp, ys = x_prompt, x_sample
    rows_p = ([], [], [], [])
    rows_s = ([], [], [], [])
    for l in range(DEPTH):
        p = dict(ln1_g=ln1_g[l], ln1_b=ln1_b[l], ffn1_wg=ffn1_wg[l], ffn1_wu=ffn1_wu[l], ffn1_wd=ffn1_wd[l],
                 w_in=w_in[l], lambda_q1=lambda_q1[l], lambda_k1=lambda_k1[l],
                 lambda_q2=lambda_q2[l], lambda_k2=lambda_k2[l], subln_g=subln_g[l], w_o=w_o[l],
                 ln2_g=ln2_g[l], ln2_b=ln2_b[l], ffn2_wg=ffn2_wg[l], ffn2_wu=ffn2_wu[l], ffn2_wd=ffn2_wd[l],
                 ln3_g=ln3_g[l], ln3_b=ln3_b[l])
        lam0 = lambda_init_for(l)
        yp, rp = encoder_layer(yp, None, None, None, None, 0, p, lam0)
        ys, rs = encoder_layer(ys, cache_sb_k[l], cache_sb_v[l], cache_diff_k[l], cache_diff_v[l],
                               PAST_LEN, p, lam0)
        for acc, r in zip(rows_p, rp):
            acc.append(r)
        for acc, r in zip(rows_s, rs):
            acc.append(r)
    new_sb_k_p = jnp.stack(rows_p[0], axis=0)
    new_sb_v_p = jnp.stack(rows_p[1], axis=0)
    new_diff_k_p = jnp.stack(rows_p[2], axis=0)
    new_diff_v_p = jnp.stack(rows_p[3], axis=0)
    new_sb_k_s = jnp.stack(rows_s[0], axis=0)
    new_sb_v_s = jnp.stack(rows_s[1], axis=0)
    new_diff_k_s = jnp.stack(rows_s[2], axis=0)
    new_diff_v_s = jnp.stack(rows_s[3], axis=0)
    return (yp, ys, new_sb_k_p, new_sb_v_p, new_diff_k_p, new_diff_v_p,
            new_sb_k_s, new_sb_v_s, new_diff_k_s, new_diff_v_s)
```

```python
import functools
import math

import jax
import jax.numpy as jnp
from jax import lax
from jax.experimental import pallas as pl
from jax.experimental.pallas import tpu as pltpu

HEAD_DIM = 64
CHUNK = 64
LANES = 128
ROPE_THETA = 10000.0
LN_EPS = 1e-5
VMEM_LIMIT_BYTES = 56 * 1024 * 1024
STICK_DEAD = -110.0
NEG_BIG = -1e30

_f32 = jnp.float32
_bf16 = jnp.bfloat16


def _layer_norm(x, g, b):
    mu = jnp.mean(x, axis=-1, keepdims=True)
    xc = x - mu
    var = jnp.mean(xc * xc, axis=-1, keepdims=True)
    return xc * lax.rsqrt(var + LN_EPS) * g + b


def _swiglu(xb, wg_ref, wu_ref, wd_ref):
    gate = jnp.dot(xb, wg_ref[...], preferred_element_type=_f32)
    up = jnp.dot(xb, wu_ref[...], preferred_element_type=_f32)
    act = (gate * jax.nn.sigmoid(gate) * up).astype(_bf16)
    return jnp.dot(act, wd_ref[...], preferred_element_type=_f32)


def _const_spec(shape):
    nd = len(shape)
    return pl.BlockSpec(shape, lambda *_: (0,) * nd, pipeline_mode=pl.Buffered(1))


def _pre_kernel(x_ref, cos_ref, sin_ref, g_ref, b_ref, wg_ref, wu_ref, wd_ref, win_ref,
                h_ref, sbk_ref, sbv_ref, dfk_ref, dfv_ref, sbq_ref, dfq_ref, *, alpha, width):
    x = x_ref[...]
    ffn = _swiglu(x.astype(_bf16), wg_ref, wu_ref, wd_ref)
    h = _layer_norm(alpha * x + 0.5 * ffn, g_ref[...], b_ref[...])
    h_ref[...] = h
    proj = jnp.dot(h.astype(_bf16), win_ref[...], preferred_element_type=_f32)
    scale = HEAD_DIM ** -0.5
    sbq_ref[...] = (proj[:, 0:width] * scale).astype(_bf16)
    sbk_ref[...] = proj[:, width:2 * width]
    sbv_ref[...] = proj[:, 2 * width:3 * width]
    dfv_ref[...] = proj[:, 5 * width:6 * width]

    cos = cos_ref[...]
    sin = sin_ref[...]
    lane = lax.broadcasted_iota(jnp.int32, cos.shape, 1)
    first_half = (lane % HEAD_DIM) < (HEAD_DIM // 2)

    def rope(xc):
        partner = jnp.where(first_half, pltpu.roll(xc, LANES - HEAD_DIM // 2, 1),
                            pltpu.roll(xc, HEAD_DIM // 2, 1))
        return xc * cos + partner * sin

    for c in range(width // LANES):
        q0 = 3 * width + c * LANES
        k0 = 4 * width + c * LANES
        dfq_ref[:, c * LANES:(c + 1) * LANES] = (rope(proj[:, q0:q0 + LANES]) * scale).astype(_bf16)
        dfk_ref[:, c * LANES:(c + 1) * LANES] = rope(proj[:, k0:k0 + LANES])


def _pre_call(x, cos_t, sin_t, ln_g, ln_b, wg, wu, wd, w_in, *, alpha, tm):
    n, d = x.shape
    width = w_in.shape[1] // 6
    nrep = cos_t.shape[0] // tm
    row = lambda w: pl.BlockSpec((tm, w), lambda i: (i, 0))
    tab = pl.BlockSpec((tm, LANES), lambda i: (i % nrep, 0))
    out_shape = ([jax.ShapeDtypeStruct((n, d), _f32)]
                 + [jax.ShapeDtypeStruct((n, width), _f32)] * 4
                 + [jax.ShapeDtypeStruct((n, width), _bf16)] * 2)
    return pl.pallas_call(
        functools.partial(_pre_kernel, alpha=alpha, width=width),
        grid=(n // tm,),
        in_specs=[row(d), tab, tab, _const_spec(ln_g.shape), _const_spec(ln_b.shape),
                  _const_spec(wg.shape), _const_spec(wu.shape), _const_spec(wd.shape),
                  _const_spec(w_in.shape)],
        out_specs=[row(d)] + [row(width)] * 6,
        out_shape=out_shape,
        compiler_params=pltpu.CompilerParams(dimension_semantics=("arbitrary",),
                                             vmem_limit_bytes=VMEM_LIMIT_BYTES),
        name="pre_ffn_ln_proj",
    )(x, cos_t, sin_t, ln_g, ln_b, wg, wu, wd, w_in)


def _post_kernel(osb_ref, odf_ref, h_ref, wo_sb_ref, wo_df_ref, g2_ref, b2_ref,
                 wg_ref, wu_ref, wd_ref, g3_ref, b3_ref, y_ref, *, alpha):
    mix = (jnp.dot(osb_ref[...], wo_sb_ref[...], preferred_element_type=_f32)
           + jnp.dot(odf_ref[...], wo_df_ref[...], preferred_element_type=_f32))
    x2 = _layer_norm(alpha * h_ref[...] + mix, g2_ref[...], b2_ref[...])
    ffn = _swiglu(x2.astype(_bf16), wg_ref, wu_ref, wd_ref)
    y_ref[...] = _layer_norm(alpha * x2 + 0.5 * ffn, g3_ref[...], b3_ref[...])


def _post_call(o_sb, o_df, h, wo_sb, wo_df, g2, b2, wg, wu, wd, g3, b3, *, alpha, tm):
    n, d = h.shape
    width = o_sb.shape[1]
    row = lambda w: pl.BlockSpec((tm, w), lambda i: (i, 0))
    consts = (wo_sb, wo_df, g2, b2, wg, wu, wd, g3, b3)
    return pl.pallas_call(
        functools.partial(_post_kernel, alpha=alpha),
        grid=(n // tm,),
        in_specs=[row(width), row(width), row(d)] + [_const_spec(c.shape) for c in consts],
        out_specs=row(d),
        out_shape=jax.ShapeDtypeStruct((n, d), _f32),
        compiler_params=pltpu.CompilerParams(dimension_semantics=("arbitrary",),
                                             vmem_limit_bytes=VMEM_LIMIT_BYTES),
        name="post_wo_ln_ffn_ln",
    )(o_sb, o_df, h, *consts)


def _dot_nt(a, b):
    return lax.dot_general(a, b, (((1,), (1,)), ((), ())), preferred_element_type=_f32)


def _cast_kv_once(k_ref, v_ref, kb_ref, vb_ref):
    @pl.when(pl.program_id(2) == 0)
    def _():
        kb_ref[...] = k_ref[0].astype(_bf16)
        vb_ref[...] = v_ref[0].astype(_bf16)


def _sb_kernel(q_ref, k_ref, v_ref, o_ref, kb_ref, vb_ref, *, pos0, tq, tk):
    _cast_kv_once(k_ref, v_ref, kb_ref, vb_ref)
    p0 = pos0 + pl.program_id(2) * tq
    kb_hi = jnp.maximum(p0 + tq - 2, 0) // tk
    kb_full = p0 // tk - 1
    q = q_ref[0]
    lane = lax.broadcasted_iota(jnp.int32, (tq, LANES), 1)
    q_pos = p0 + lax.broadcasted_iota(jnp.int32, (tq, tk), 0)
    k_iota = lax.broadcasted_iota(jnp.int32, (tq, tk), 1)
    upper = (lax.broadcasted_iota(jnp.int32, (tk, tk), 0)
             > lax.broadcasted_iota(jnp.int32, (tk, tk), 1)).astype(_bf16)

    def tile(qz, kb, carry, acc, masked):
        start = pl.multiple_of(kb * tk, tk)
        kt = kb_ref[pl.ds(start, tk), :]
        vt = vb_ref[pl.ds(start, tk), :]
        z = _dot_nt(qz, kt)
        nz = -z
        l = jnp.minimum(nz, 0.0) - jnp.log(1.0 + jnp.exp(jnp.minimum(z, nz)))
        if masked:
            visible = (start + k_iota) < q_pos
            l = jnp.where(visible, l, 0.0)
        l_hi = l.astype(_bf16)
        l_lo = (l - l_hi.astype(_f32)).astype(_bf16)
        cum = (jnp.dot(l_hi, upper, preferred_element_type=_f32)
               + jnp.dot(l_lo, upper, preferred_element_type=_f32))
        w = jnp.exp(z + l + cum + carry)
        if masked:
            w = jnp.where(visible, w, 0.0)
        acc = acc + jnp.dot(w.astype(_bf16), vt, preferred_element_type=_f32)
        carry = carry + jnp.sum(l, axis=1, keepdims=True)
        return carry, acc

    def one_head(in_head):
        qz = jnp.where(in_head, q, jnp.zeros_like(q))
        carry = jnp.zeros((tq, 1), _f32)
        acc = jnp.zeros((tq, LANES), _f32)

        def masked_body(i, st):
            return tile(qz, kb_hi - i, st[0], st[1], True)

        carry, acc = lax.fori_loop(0, kb_hi - kb_full, masked_body, (carry, acc))

        def cond(st):
            kb, alive, _, _ = st
            return jnp.logical_and(kb >= 0, alive > STICK_DEAD)

        def body(st):
            kb, _, carry, acc = st
            carry, acc = tile(qz, kb, carry, acc, False)
            return kb - 1, jnp.max(carry), carry, acc

        _, _, _, acc = lax.while_loop(cond, body, (kb_full, jnp.max(carry), carry, acc))
        return acc

    acc0 = one_head(lane < HEAD_DIM)
    acc1 = one_head(lane >= HEAD_DIM)
    o_ref[0] = jnp.where(lane < HEAD_DIM, acc0, acc1).astype(o_ref.dtype)


def _sb_call(q, k, v, *, pos0, tq, tk):
    b, t, width = q.shape
    tkv = k.shape[1]
    grid = (b, width // LANES, t // tq)
    qspec = pl.BlockSpec((1, tq, LANES), lambda bi, hi, qi: (bi, qi, hi))
    kvspec = pl.BlockSpec((1, tkv, LANES), lambda bi, hi, qi: (bi, 0, hi))
    return pl.pallas_call(
        functools.partial(_sb_kernel, pos0=pos0, tq=tq, tk=tk),
        grid=grid,
        in_specs=[qspec, kvspec, kvspec],
        out_specs=qspec,
        out_shape=jax.ShapeDtypeStruct((b, t, width), _bf16),
        scratch_shapes=[pltpu.VMEM((tkv, LANES), _bf16), pltpu.VMEM((tkv, LANES), _bf16)],
        compiler_params=pltpu.CompilerParams(
            dimension_semantics=("arbitrary", "arbitrary", "arbitrary"),
            vmem_limit_bytes=VMEM_LIMIT_BYTES),
        name="stick_breaking_attention",
    )(q, k, v)


def _diff_kernel(q_ref, k_ref, v_ref, lq1_ref, lk1_ref, lq2_ref, lk2_ref, g_ref, o_ref,
                 kb_ref, vb_ref, *, pos0, tq, tk, lambda_init):
    _cast_kv_once(k_ref, v_ref, kb_ref, vb_ref)
    p0 = pos0 + pl.program_id(2) * tq
    vis_end_first = (p0 // CHUNK + 1) * CHUNK
    vis_end_last = ((p0 + tq - 1) // CHUNK + 1) * CHUNK
    kb_hi = (vis_end_last - 1) // tk
    kb_full = vis_end_first // tk - 1
    q = q_ref[0]
    lane = lax.broadcasted_iota(jnp.int32, (tq, LANES), 1)
    qz = (jnp.where(lane < HEAD_DIM, q, jnp.zeros_like(q)),
          jnp.where(lane >= HEAD_DIM, q, jnp.zeros_like(q)))
    q_chunk = (p0 + lax.broadcasted_iota(jnp.int32, (tq, tk), 0)) // CHUNK
    k_iota = lax.broadcasted_iota(jnp.int32, (tq, tk), 1)

    def tile(kb, st, masked):
        start = pl.multiple_of(kb * tk, tk)
        kt = kb_ref[pl.ds(start, tk), :]
        vt = vb_ref[pl.ds(start, tk), :]
        if masked:
            visible = (start + k_iota) // CHUNK <= q_chunk
        new = []
        for c in range(2):
            m, den, acc = st[3 * c:3 * c + 3]
            s = _dot_nt(qz[c], kt)
            if masked:
                s = jnp.where(visible, s, NEG_BIG)
            m_new = jnp.maximum(m, jnp.max(s, axis=1, keepdims=True))
            p = jnp.exp(s - m_new)
            if masked:
                p = jnp.where(visible, p, 0.0)
            a = jnp.exp(m - m_new)
            den = a * den + jnp.sum(p, axis=1, keepdims=True)
            acc = a * acc + jnp.dot(p.astype(_bf16), vt, preferred_element_type=_f32)
            new += [m_new, den, acc]
        return tuple(new)

    st = (jnp.full((tq, 1), NEG_BIG, _f32), jnp.zeros((tq, 1), _f32), jnp.zeros((tq, LANES), _f32)) * 2
    st = lax.fori_loop(0, kb_hi - kb_full, lambda i, s: tile(kb_hi - i, s, True), st)
    st = lax.fori_loop(0, kb_full + 1, lambda i, s: tile(kb_full - i, s, False), st)

    lam = (jnp.exp(jnp.sum(lq1_ref[...] * lk1_ref[...], axis=1, keepdims=True))
           - jnp.exp(jnp.sum(lq2_ref[...] * lk2_ref[...], axis=1, keepdims=True)) + lambda_init)
    o = st[2] / st[1] - lam * (st[5] / st[4])
    o = o * lax.rsqrt(jnp.mean(o * o, axis=1, keepdims=True) + LN_EPS)
    o_ref[0] = (o * g_ref[...] * (1.0 - lambda_init)).astype(o_ref.dtype)


def _diff_call(q, k, v, lq1, lk1, lq2, lk2, subln_g, *, pos0, tq, tk, lambda_init):
    b, t, width = q.shape
    tkv = k.shape[1]
    grid = (b, width // LANES, t // tq)
    qspec = pl.BlockSpec((1, tq, LANES), lambda bi, hi, qi: (bi, qi, hi))
    kvspec = pl.BlockSpec((1, tkv, LANES), lambda bi, hi, qi: (bi, 0, hi))
    small = (lq1, lk1, lq2, lk2, subln_g)
    return pl.pallas_call(
        functools.partial(_diff_kernel, pos0=pos0, tq=tq, tk=tk, lambda_init=lambda_init),
        grid=grid,
        in_specs=[qspec, kvspec, kvspec] + [_const_spec(s.shape) for s in small],
        out_specs=qspec,
        out_shape=jax.ShapeDtypeStruct((b, t, width), _bf16),
        scratch_shapes=[pltpu.VMEM((tkv, LANES), _bf16), pltpu.VMEM((tkv, LANES), _bf16)],
        compiler_params=pltpu.CompilerParams(
            dimension_semantics=("arbitrary", "arbitrary", "arbitrary"),
            vmem_limit_bytes=VMEM_LIMIT_BYTES),
        name="differential_attention",
    )(q, k, v, *small)


def _rope_tables(pos, reps):
    half = HEAD_DIM // 2
    inv_freq = ROPE_THETA ** (-jnp.arange(half, dtype=_f32) / half)
    ang = pos.astype(_f32)[:, None] * inv_freq[None, :]
    cos, sin = jnp.cos(ang), jnp.sin(ang)
    cos_t = jnp.tile(jnp.concatenate([cos, cos], axis=1), (reps, LANES // HEAD_DIM))
    sin_t = jnp.tile(jnp.concatenate([-sin, sin], axis=1), (reps, LANES // HEAD_DIM))
    return cos_t, sin_t


def _layer(x, caches, pos0, p, lambda_init, alpha, *, tm, tq, tk):
    b, t, d = x.shape
    n = b * t
    width = p["w_in"].shape[1] // 6
    tm = min(tm, n)
    if t >= tm:
        cos_t, sin_t = _rope_tables(pos0 + jnp.arange(t), 1)
    else:
        cos_t, sin_t = _rope_tables(pos0 + jnp.arange(t), tm // t)
    h, sbk, sbv, dfk, dfv, sbq, dfq = _pre_call(
        x.reshape(n, d), cos_t, sin_t, p["ln1_g"], p["ln1_b"],
        p["ffn1_wg"], p["ffn1_wu"], p["ffn1_wd"], p["w_in"], alpha=alpha, tm=tm)
    rows = (sbk, sbv, dfk, dfv)
    r3 = lambda a: a.reshape(b, t, width)
    if caches is None:
        k_sb, v_sb, k_df, v_df = (r3(a) for a in rows)
    else:
        tkv = caches[0].shape[1] + t
        pad = (-tkv) % tk
        cat = lambda c, a: jnp.concatenate(
            [c.reshape(b, -1, width), r3(a), jnp.zeros((b, pad, width), _f32)], axis=1)
        k_sb, v_sb, k_df, v_df = (cat(c, a) for c, a in zip(caches, rows))
    o_sb = _sb_call(r3(sbq), k_sb, v_sb, pos0=pos0, tq=tq, tk=tk)
    o_df = _diff_call(r3(dfq), k_df, v_df, p["lambda_q1"], p["lambda_k1"], p["lambda_q2"],
                      p["lambda_k2"], p["subln_g"], pos0=pos0, tq=tq, tk=tk, lambda_init=lambda_init)
    y = _post_call(o_sb.reshape(n, width), o_df.reshape(n, width), h, p["wo_sb"], p["wo_df"],
                   p["ln2_g"], p["ln2_b"], p["ffn2_wg"], p["ffn2_wu"], p["ffn2_wd"],
                   p["ln3_g"], p["ln3_b"], alpha=alpha, tm=tm)
    return y.reshape(b, t, d), rows


def _run(x_prompt, x_sample, caches, params, *, depth, tm, tq_prompt, tk):
    assert depth == 1, "the projection / attention layouts assume a single layer"
    alpha = (2 * depth) ** 0.25
    lambda_init = 0.8 - 0.6 * math.exp(-0.3 * 0)
    width = params["w_in"].shape[-1] // 6
    p = {}
    for name, val in params.items():
        val = val[0]
        if name.startswith("ffn") or name == "w_in":
            p[name] = val.astype(_bf16)
        elif name == "w_o":
            p["wo_sb"] = val[:width].astype(_bf16)
            p["wo_df"] = val[width:].astype(_bf16)
        else:
            p[name] = val.reshape(1, -1)
    yp, rp = _layer(x_prompt, None, 0, p, lambda_init, alpha, tm=tm, tq=tq_prompt, tk=tk)
    past = caches[0].shape[2]
    cs = tuple(c[0] for c in caches)
    ys, rs = _layer(x_sample, cs, past, p, lambda_init, alpha, tm=tm, tq=x_sample.shape[1], tk=tk)
    return yp, ys, rp, rs


def kernel(x_prompt, x_sample, cache_sb_k, cache_sb_v, cache_diff_k, cache_diff_v, ln1_g, ln1_b, ffn1_wg, ffn1_wu, ffn1_wd, w_in, lambda_q1, lambda_k1, lambda_q2, lambda_k2, subln_g, w_o, ln2_g, ln2_b, ffn2_wg, ffn2_wu, ffn2_wd, ln3_g, ln3_b):
    params = dict(ln1_g=ln1_g, ln1_b=ln1_b, ffn1_wg=ffn1_wg, ffn1_wu=ffn1_wu, ffn1_wd=ffn1_wd,
                  w_in=w_in, lambda_q1=lambda_q1, lambda_k1=lambda_k1, lambda_q2=lambda_q2,
                  lambda_k2=lambda_k2, subln_g=subln_g, w_o=w_o, ln2_g=ln2_g, ln2_b=ln2_b,
                  ffn2_wg=ffn2_wg, ffn2_wu=ffn2_wu, ffn2_wd=ffn2_wd, ln3_g=ln3_g, ln3_b=ln3_b)
    depth = w_in.shape[0]
    yp, ys, rp, rs = _run(x_prompt, x_sample, (cache_sb_k, cache_sb_v, cache_diff_k, cache_diff_v),
                          params, depth=depth, tm=512, tq_prompt=256, tk=256)
    bp, tp = x_prompt.shape[:2]
    bs, ts = x_sample.shape[:2]
    heads = rp[0].shape[-1] // HEAD_DIM

    def shapes(rows, b, t):
        sbk, sbv, dfk, dfv = rows
        return (sbk.reshape(depth, b, t, heads, HEAD_DIM), sbv.reshape(depth, b, t, heads, HEAD_DIM),
                dfk.reshape(depth, b, t, heads // 2, 2, HEAD_DIM),
                dfv.reshape(depth, b, t, heads // 2, 2 * HEAD_DIM))

    return (yp, ys) + shapes(rp, bp, tp) + shapes(rs, bs, ts)
```

```python
import functools
import math

import jax
import jax.numpy as jnp
from jax import lax
from jax.experimental import pallas as pl
from jax.experimental.pallas import tpu as pltpu

HEAD_DIM = 64
CHUNK = 64
LANES = 128
STRIP = 16
ROPE_THETA = 10000.0
LN_EPS = 1e-5
VMEM_LIMIT_BYTES = 56 * 1024 * 1024
STICK_DEAD = -160.0
NEG_BIG = -1e30
LOG2E = math.log2(math.e)

_f32 = jnp.float32
_bf16 = jnp.bfloat16


def _layer_norm(x, g, b):
    mu = jnp.mean(x, axis=-1, keepdims=True)
    xc = x - mu
    var = jnp.mean(xc * xc, axis=-1, keepdims=True)
    return xc * lax.rsqrt(var + LN_EPS) * g + b


def _swiglu(xb, wg_ref, wu_ref, wd_ref):
    gate = jnp.dot(xb, wg_ref[...], preferred_element_type=_f32)
    up = jnp.dot(xb, wu_ref[...], preferred_element_type=_f32)
    act = (gate * jax.nn.sigmoid(gate) * up).astype(_bf16)
    return jnp.dot(act, wd_ref[...], preferred_element_type=_f32)


def _const_spec(shape):
    nd = len(shape)
    return pl.BlockSpec(shape, lambda *_: (0,) * nd, pipeline_mode=pl.Buffered(1))


def _pre_kernel(x_ref, cos_ref, sin_ref, g_ref, b_ref, wg_ref, wu_ref, wd_ref, win_ref,
                h_ref, sbk_ref, sbv_ref, dfk_ref, dfv_ref, sbq_ref, dfq_ref,
                sbkb_ref, sbvb_ref, dfkb_ref, dfvb_ref, *, alpha, width):
    x = x_ref[...]
    ffn = _swiglu(x.astype(_bf16), wg_ref, wu_ref, wd_ref)
    h = _layer_norm(alpha * x + 0.5 * ffn, g_ref[...], b_ref[...])
    h_ref[...] = h
    proj = jnp.dot(h.astype(_bf16), win_ref[...], preferred_element_type=_f32)
    scale = HEAD_DIM ** -0.5 * LOG2E
    sbq_ref[...] = (proj[:, 0:width] * scale).astype(_bf16)
    sbk_ref[...] = proj[:, width:2 * width]
    sbkb_ref[...] = proj[:, width:2 * width].astype(_bf16)
    sbv_ref[...] = proj[:, 2 * width:3 * width]
    sbvb_ref[...] = proj[:, 2 * width:3 * width].astype(_bf16)
    dfv_ref[...] = proj[:, 5 * width:6 * width]
    dfvb_ref[...] = proj[:, 5 * width:6 * width].astype(_bf16)

    cos = cos_ref[...]
    sin = sin_ref[...]
    lane = lax.broadcasted_iota(jnp.int32, cos.shape, 1)
    first_half = (lane % HEAD_DIM) < (HEAD_DIM // 2)

    def rope(xc):
        partner = jnp.where(first_half, pltpu.roll(xc, LANES - HEAD_DIM // 2, 1),
                            pltpu.roll(xc, HEAD_DIM // 2, 1))
        return xc * cos + partner * sin

    for c in range(width // LANES):
        q0 = 3 * width + c * LANES
        k0 = 4 * width + c * LANES
        dfq_ref[:, c * LANES:(c + 1) * LANES] = (rope(proj[:, q0:q0 + LANES]) * scale).astype(_bf16)
        dfk = rope(proj[:, k0:k0 + LANES])
        dfk_ref[:, c * LANES:(c + 1) * LANES] = dfk
        dfkb_ref[:, c * LANES:(c + 1) * LANES] = dfk.astype(_bf16)


def _pre_call(x, cos_t, sin_t, ln_g, ln_b, wg, wu, wd, w_in, *, alpha, tm):
    n, d = x.shape
    width = w_in.shape[1] // 6
    nrep = cos_t.shape[0] // tm
    row = lambda w: pl.BlockSpec((tm, w), lambda i: (i, 0))
    tab = pl.BlockSpec((tm, LANES), lambda i: (i % nrep, 0))
    out_shape = ([jax.ShapeDtypeStruct((n, d), _f32)]
                 + [jax.ShapeDtypeStruct((n, width), _f32)] * 4
                 + [jax.ShapeDtypeStruct((n, width), _bf16)] * 6)
    return pl.pallas_call(
        functools.partial(_pre_kernel, alpha=alpha, width=width),
        grid=(n // tm,),
        in_specs=[row(d), tab, tab, _const_spec(ln_g.shape), _const_spec(ln_b.shape),
                  _const_spec(wg.shape), _const_spec(wu.shape), _const_spec(wd.shape),
                  _const_spec(w_in.shape)],
        out_specs=[row(d)] + [row(width)] * 10,
        out_shape=out_shape,
        compiler_params=pltpu.CompilerParams(dimension_semantics=("arbitrary",),
                                             vmem_limit_bytes=VMEM_LIMIT_BYTES),
        name="pre_ffn_ln_proj",
    )(x, cos_t, sin_t, ln_g, ln_b, wg, wu, wd, w_in)


def _post_kernel(osb_ref, odf_ref, h_ref, wo_sb_ref, wo_df_ref, g2_ref, b2_ref,
                 wg_ref, wu_ref, wd_ref, g3_ref, b3_ref, y_ref, *, alpha):
    mix = (jnp.dot(osb_ref[...], wo_sb_ref[...], preferred_element_type=_f32)
           + jnp.dot(odf_ref[...], wo_df_ref[...], preferred_element_type=_f32))
    x2 = _layer_norm(alpha * h_ref[...] + mix, g2_ref[...], b2_ref[...])
    ffn = _swiglu(x2.astype(_bf16), wg_ref, wu_ref, wd_ref)
    y_ref[...] = _layer_norm(alpha * x2 + 0.5 * ffn, g3_ref[...], b3_ref[...])


def _post_call(o_sb, o_df, h, wo_sb, wo_df, g2, b2, wg, wu, wd, g3, b3, *, alpha, tm):
    n, d = h.shape
    width = o_sb.shape[1]
    row = lambda w: pl.BlockSpec((tm, w), lambda i: (i, 0))
    consts = (wo_sb, wo_df, g2, b2, wg, wu, wd, g3, b3)
    return pl.pallas_call(
        functools.partial(_post_kernel, alpha=alpha),
        grid=(n // tm,),
        in_specs=[row(width), row(width), row(d)] + [_const_spec(c.shape) for c in consts],
        out_specs=row(d),
        out_shape=jax.ShapeDtypeStruct((n, d), _f32),
        compiler_params=pltpu.CompilerParams(dimension_semantics=("arbitrary",),
                                             vmem_limit_bytes=VMEM_LIMIT_BYTES),
        name="post_wo_ln_ffn_ln",
    )(o_sb, o_df, h, *consts)


def _dot_nt(a, b):
    return lax.dot_general(a, b, (((1,), (1,)), ((), ())), preferred_element_type=_f32)


def _stack_halves(q):
    lane = lax.broadcasted_iota(jnp.int32, q.shape, 1)
    zero = jnp.zeros_like(q)
    return jnp.concatenate([jnp.where(lane < HEAD_DIM, q, zero), jnp.where(lane >= HEAD_DIM, q, zero)], axis=0)


def _sb_kernel(q_ref, k_ref, v_ref, o_ref, qs_ref, upper_ref, carry_ref, acc_ref, *, pos0, tq, tk):
    pairs = q_ref.shape[2] // LANES
    p0 = pos0 + pl.program_id(1) * tq
    kb_hi = jnp.maximum(p0 + tq - 2, 0) // tk
    kb_full = p0 // tk - 1
    for h in range(pairs):
        qs_ref[h] = _stack_halves(q_ref[0, :, h * LANES:(h + 1) * LANES])
    upper_ref[...] = (lax.broadcasted_iota(jnp.int32, (tk, tk), 0)
                      > lax.broadcasted_iota(jnp.int32, (tk, tk), 1)).astype(_bf16)
    carry_ref[...] = jnp.zeros(carry_ref.shape, _f32)
    acc_ref[...] = jnp.zeros(acc_ref.shape, _f32)
    col_minus_row = (lax.broadcasted_iota(jnp.int32, (STRIP, LANES), 1)
                     - lax.broadcasted_iota(jnp.int32, (STRIP, LANES), 0))
    strips = range(0, 2 * tq, STRIP)
    lane_blocks = range(0, tk, LANES)

    def tile(kb, masked):
        start = pl.multiple_of(kb * tk, tk)

        def visible(r):
            limit = p0 + (r % tq) - start
            return [col_minus_row < limit - c for c in lane_blocks]

        for h in range(pairs):
            hl = slice(h * LANES, (h + 1) * LANES)
            z = _dot_nt(qs_ref[h], k_ref[0, pl.ds(start, tk), hl])
            lbs, zls, tots = [], [], []
            for r in strips:
                zc = [z[r:r + STRIP, c:c + LANES] for c in lane_blocks]
                zl = [jnp.minimum(x, 0.0) - jnp.log2(1.0 + jnp.exp2(jnp.minimum(x, -x))) for x in zc]
                ls = [a - x for a, x in zip(zl, zc)]
                if masked:
                    ls = [jnp.where(v, l, 0.0) for v, l in zip(visible(r), ls)]
                zls.append(zl)
                lbs.append(jnp.concatenate([l.astype(_bf16) for l in ls], axis=1))
                tots.append(jnp.sum(functools.reduce(jnp.add, ls), axis=1, keepdims=True))
            cum = jnp.dot(jnp.concatenate(lbs, axis=0), upper_ref[...], preferred_element_type=_f32)
            ws = []
            for i, r in enumerate(strips):
                carry = carry_ref[h, r:r + STRIP, :]
                w = [jnp.exp2(zl + cum[r:r + STRIP, c:c + LANES] + carry) for zl, c in zip(zls[i], lane_blocks)]
                if masked:
                    w = [jnp.where(v, x, 0.0) for v, x in zip(visible(r), w)]
                carry_ref[h, r:r + STRIP, :] = carry + tots[i]
                ws.append(jnp.concatenate([x.astype(_bf16) for x in w], axis=1))
            acc_ref[h] += jnp.dot(jnp.concatenate(ws, axis=0), v_ref[0, pl.ds(start, tk), hl],
                                  preferred_element_type=_f32)

    @pl.loop(0, kb_hi - kb_full)
    def _(i):
        tile(kb_hi - i, True)

    def alive(st):
        kb, least_dead = st
        return jnp.logical_and(kb >= 0, least_dead > STICK_DEAD)

    def walk(st):
        kb, _ = st
        tile(kb, False)
        return kb - 1, jnp.max(carry_ref[...])

    lax.while_loop(alive, walk, (kb_full, jnp.max(carry_ref[...])))

    lane = lax.broadcasted_iota(jnp.int32, (tq, LANES), 1)
    for h in range(pairs):
        o_ref[0, :, h * LANES:(h + 1) * LANES] = jnp.where(
            lane < HEAD_DIM, acc_ref[h, 0:tq, :], acc_ref[h, tq:2 * tq, :]).astype(o_ref.dtype)


def _sb_call(q, k, v, *, pos0, tq, tk):
    b, t, width = q.shape
    tkv = k.shape[1]
    pairs = width // LANES
    qspec = pl.BlockSpec((1, tq, width), lambda bi, qi: (bi, qi, 0))
    kvspec = pl.BlockSpec((1, tkv, width), lambda bi, qi: (bi, 0, 0), pipeline_mode=pl.Buffered(1))
    return pl.pallas_call(
        functools.partial(_sb_kernel, pos0=pos0, tq=tq, tk=tk),
        grid=(b, t // tq),
        in_specs=[qspec, kvspec, kvspec],
        out_specs=qspec,
        out_shape=jax.ShapeDtypeStruct((b, t, width), _bf16),
        scratch_shapes=[pltpu.VMEM((pairs, 2 * tq, LANES), _bf16),
                        pltpu.VMEM((tk, tk), _bf16),
                        pltpu.VMEM((pairs, 2 * tq, LANES), _f32),
                        pltpu.VMEM((pairs, 2 * tq, LANES), _f32)],
        compiler_params=pltpu.CompilerParams(
            dimension_semantics=("arbitrary", "arbitrary"),
            vmem_limit_bytes=VMEM_LIMIT_BYTES),
        name="stick_breaking_attention",
    )(q, k, v)


def _diff_kernel(q_ref, k_ref, v_ref, lq1_ref, lk1_ref, lq2_ref, lk2_ref, g_ref, o_ref,
                 qs_ref, m_ref, acc_ref, *, pos0, tq, tk, lambda_init):
    heads = q_ref.shape[2] // LANES
    p0 = pos0 + pl.program_id(1) * tq
    vis_end_first = (p0 // CHUNK + 1) * CHUNK
    vis_end_last = ((p0 + tq - 1) // CHUNK + 1) * CHUNK
    kb_hi = (vis_end_last - 1) // tk
    kb_full = vis_end_first // tk - 1
    for h in range(heads):
        qs_ref[h] = _stack_halves(q_ref[0, :, h * LANES:(h + 1) * LANES])
    m_ref[...] = jnp.full(m_ref.shape, NEG_BIG, _f32)
    acc_ref[...] = jnp.zeros(acc_ref.shape, _f32)
    col = lax.broadcasted_iota(jnp.int32, (STRIP, LANES), 1)
    ones = jnp.ones((tk, LANES), _bf16)

    def tile(kb, masked):
        start = pl.multiple_of(kb * tk, tk)
        for h in range(heads):
            hl = slice(h * LANES, (h + 1) * LANES)
            s = _dot_nt(qs_ref[h], k_ref[0, pl.ds(start, tk), hl])
            probs = []
            for r in range(0, 2 * tq, STRIP):
                rows = slice(r, r + STRIP)
                cols = [s[rows, c:c + LANES] for c in range(0, tk, LANES)]
                if masked:
                    limit = ((r % tq) // CHUNK + 1) * CHUNK + p0 - start
                    vis = [col + c < limit for c in range(0, tk, LANES)]
                    cols = [jnp.where(v, x, NEG_BIG) for v, x in zip(vis, cols)]
                m_old = m_ref[h, rows, :]
                m_new = jnp.maximum(m_old, jnp.max(functools.reduce(jnp.maximum, cols), axis=1, keepdims=True))
                ps = [jnp.exp2(x - m_new) for x in cols]
                if masked:
                    ps = [jnp.where(v, x, 0.0) for v, x in zip(vis, ps)]
                alpha = jnp.exp2(m_old - m_new)
                m_ref[h, rows, :] = m_new
                acc_ref[h, rows, 0:LANES] = alpha * acc_ref[h, rows, 0:LANES]
                acc_ref[h, rows, LANES:2 * LANES] = alpha * acc_ref[h, rows, LANES:2 * LANES]
                probs.append(jnp.concatenate([x.astype(_bf16) for x in ps], axis=1))
            p = jnp.concatenate(probs, axis=0)
            v1 = jnp.concatenate([v_ref[0, pl.ds(start, tk), hl], ones], axis=1)
            acc_ref[h] += jnp.dot(p, v1, preferred_element_type=_f32)

    @pl.loop(0, kb_hi - kb_full)
    def _(i):
        tile(kb_hi - i, True)

    @pl.loop(0, kb_full + 1)
    def _(i):
        tile(kb_full - i, False)

    lam = (jnp.exp(jnp.sum(lq1_ref[...] * lk1_ref[...], axis=1, keepdims=True))
           - jnp.exp(jnp.sum(lq2_ref[...] * lk2_ref[...], axis=1, keepdims=True)) + lambda_init)
    for h in range(heads):
        o = (acc_ref[h, 0:tq, 0:LANES] / acc_ref[h, 0:tq, LANES:2 * LANES]
             - lam * (acc_ref[h, tq:2 * tq, 0:LANES] / acc_ref[h, tq:2 * tq, LANES:2 * LANES]))
        o = o * lax.rsqrt(jnp.mean(o * o, axis=1, keepdims=True) + LN_EPS)
        o_ref[0, :, h * LANES:(h + 1) * LANES] = (o * g_ref[...] * (1.0 - lambda_init)).astype(o_ref.dtype)


def _diff_call(q, k, v, lq1, lk1, lq2, lk2, subln_g, *, pos0, tq, tk, lambda_init):
    b, t, width = q.shape
    tkv = k.shape[1]
    heads = width // LANES
    assert tq % CHUNK == 0 and pos0 % CHUNK == 0 and CHUNK % STRIP == 0
    qspec = pl.BlockSpec((1, tq, width), lambda bi, qi: (bi, qi, 0))
    kvspec = pl.BlockSpec((1, tkv, width), lambda bi, qi: (bi, 0, 0), pipeline_mode=pl.Buffered(1))
    small = (lq1, lk1, lq2, lk2, subln_g)
    return pl.pallas_call(
        functools.partial(_diff_kernel, pos0=pos0, tq=tq, tk=tk, lambda_init=lambda_init),
        grid=(b, t // tq),
        in_specs=[qspec, kvspec, kvspec] + [_const_spec(s.shape) for s in small],
        out_specs=qspec,
        out_shape=jax.ShapeDtypeStruct((b, t, width), _bf16),
        scratch_shapes=[pltpu.VMEM((heads, 2 * tq, LANES), _bf16),
                        pltpu.VMEM((heads, 2 * tq, LANES), _f32),
                        pltpu.VMEM((heads, 2 * tq, 2 * LANES), _f32)],
        compiler_params=pltpu.CompilerParams(
            dimension_semantics=("arbitrary", "arbitrary"),
            vmem_limit_bytes=VMEM_LIMIT_BYTES),
        name="differential_attention",
    )(q, k, v, *small)


def _rope_tables(pos, reps):
    half = HEAD_DIM // 2
    inv_freq = ROPE_THETA ** (-jnp.arange(half, dtype=_f32) / half)
    ang = pos.astype(_f32)[:, None] * inv_freq[None, :]
    cos, sin = jnp.cos(ang), jnp.sin(ang)
    cos_t = jnp.tile(jnp.concatenate([cos, cos], axis=1), (reps, LANES // HEAD_DIM))
    sin_t = jnp.tile(jnp.concatenate([-sin, sin], axis=1), (reps, LANES // HEAD_DIM))
    return cos_t, sin_t


def _layer(x, caches, pos0, p, lambda_init, alpha, *, tm, tq, tk):
    b, t, d = x.shape
    n = b * t
    width = p["w_in"].shape[1] // 6
    tm = min(tm, n)
    if t >= tm:
        cos_t, sin_t = _rope_tables(pos0 + jnp.arange(t), 1)
    else:
        cos_t, sin_t = _rope_tables(pos0 + jnp.arange(t), tm // t)
    h, sbk, sbv, dfk, dfv, sbq, dfq, sbkb, sbvb, dfkb, dfvb = _pre_call(
        x.reshape(n, d), cos_t, sin_t, p["ln1_g"], p["ln1_b"],
        p["ffn1_wg"], p["ffn1_wu"], p["ffn1_wd"], p["w_in"], alpha=alpha, tm=tm)
    rows = (sbk, sbv, dfk, dfv)
    r3 = lambda a: a.reshape(b, t, width)
    if caches is None:
        k_sb, v_sb, k_df, v_df = r3(sbkb), r3(sbvb), r3(dfkb), r3(dfvb)
    else:
        tkv = caches[0].shape[1] + t
        pad = (-tkv) % tk
        cat = lambda c, a: jnp.concatenate(
            [c.reshape(b, -1, width).astype(a.dtype), r3(a), jnp.zeros((b, pad, width), a.dtype)], axis=1)
        k_sb, v_sb, k_df, v_df = (cat(c, a) for c, a in zip(caches, (sbkb, sbvb, dfkb, dfvb)))
    o_sb = _sb_call(r3(sbq), k_sb, v_sb, pos0=pos0, tq=tq, tk=tk)
    o_df = _diff_call(r3(dfq), k_df, v_df, p["lambda_q1"], p["lambda_k1"], p["lambda_q2"],
                      p["lambda_k2"], p["subln_g"], pos0=pos0, tq=tq, tk=tk, lambda_init=lambda_init)
    y = _post_call(o_sb.reshape(n, width), o_df.reshape(n, width), h, p["wo_sb"], p["wo_df"],
                   p["ln2_g"], p["ln2_b"], p["ffn2_wg"], p["ffn2_wu"], p["ffn2_wd"],
                   p["ln3_g"], p["ln3_b"], alpha=alpha, tm=tm)
    return y.reshape(b, t, d), rows


def _run(x_prompt, x_sample, caches, params, *, depth, tm, tq_prompt, tk):
    assert depth == 1, "the projection / attention layouts assume a single layer"
    alpha = (2 * depth) ** 0.25
    lambda_init = 0.8 - 0.6 * math.exp(-0.3 * 0)
    width = params["w_in"].shape[-1] // 6
    p = {}
    for name, val in params.items():
        val = val[0]
        if name.startswith("ffn") or name == "w_in":
            p[name] = val.astype(_bf16)
        elif name == "w_o":
            p["wo_sb"] = val[:width].astype(_bf16)
            p["wo_df"] = val[width:].astype(_bf16)
        else:
            p[name] = val.reshape(1, -1)
    yp, rp = _layer(x_prompt, None, 0, p, lambda_init, alpha, tm=tm, tq=tq_prompt, tk=tk)
    past = caches[0].shape[2]
    cs = tuple(c[0] for c in caches)
    ys, rs = _layer(x_sample, cs, past, p, lambda_init, alpha, tm=tm, tq=x_sample.shape[1], tk=tk)
    return yp, ys, rp, rs


def kernel(x_prompt, x_sample, cache_sb_k, cache_sb_v, cache_diff_k, cache_diff_v, ln1_g, ln1_b, ffn1_wg, ffn1_wu, ffn1_wd, w_in, lambda_q1, lambda_k1, lambda_q2, lambda_k2, subln_g, w_o, ln2_g, ln2_b, ffn2_wg, ffn2_wu, ffn2_wd, ln3_g, ln3_b):
    params = dict(ln1_g=ln1_g, ln1_b=ln1_b, ffn1_wg=ffn1_wg, ffn1_wu=ffn1_wu, ffn1_wd=ffn1_wd,
                  w_in=w_in, lambda_q1=lambda_q1, lambda_k1=lambda_k1, lambda_q2=lambda_q2,
                  lambda_k2=lambda_k2, subln_g=subln_g, w_o=w_o, ln2_g=ln2_g, ln2_b=ln2_b,
                  ffn2_wg=ffn2_wg, ffn2_wu=ffn2_wu, ffn2_wd=ffn2_wd, ln3_g=ln3_g, ln3_b=ln3_b)
    depth = w_in.shape[0]
    yp, ys, rp, rs = _run(x_prompt, x_sample, (cache_sb_k, cache_sb_v, cache_diff_k, cache_diff_v),
                          params, depth=depth, tm=512, tq_prompt=256, tk=256)
    bp, tp = x_prompt.shape[:2]
    bs, ts = x_sample.shape[:2]
    heads = rp[0].shape[-1] // HEAD_DIM

    def shapes(rows, b, t):
        sbk, sbv, dfk, dfv = rows
        return (sbk.reshape(depth, b, t, heads, HEAD_DIM), sbv.reshape(depth, b, t, heads, HEAD_DIM),
                dfk.reshape(depth, b, t, heads // 2, 2, HEAD_DIM),
                dfv.reshape(depth, b, t, heads // 2, 2 * HEAD_DIM))

    return (yp, ys) + shapes(rp, bp, tp) + shapes(rs, bs, ts)
```

```python
import functools
import math

import jax
import jax.numpy as jnp
from jax import lax
from jax.experimental import pallas as pl
from jax.experimental.pallas import tpu as pltpu

HEAD_DIM = 64
CHUNK = 64
LANES = 128
STRIP = 16
ROPE_THETA = 10000.0
LN_EPS = 1e-5
VMEM_LIMIT_BYTES = 56 * 1024 * 1024
STICK_DEAD = -160.0
NEG_BIG = -1e30
LOG2E = math.log2(math.e)

_f32 = jnp.float32
_bf16 = jnp.bfloat16


def _layer_norm(x, g, b):
    mu = jnp.mean(x, axis=-1, keepdims=True)
    xc = x - mu
    var = jnp.mean(xc * xc, axis=-1, keepdims=True)
    return xc * lax.rsqrt(var + LN_EPS) * g + b


def _swiglu(xb, wg_ref, wu_ref, wd_ref):
    gate = jnp.dot(xb, wg_ref[...], preferred_element_type=_f32)
    up = jnp.dot(xb, wu_ref[...], preferred_element_type=_f32)
    act = (gate * jax.nn.sigmoid(gate) * up).astype(_bf16)
    return jnp.dot(act, wd_ref[...], preferred_element_type=_f32)


def _const_spec(shape):
    nd = len(shape)
    return pl.BlockSpec(shape, lambda *_: (0,) * nd, pipeline_mode=pl.Buffered(1))


def _pre_kernel(x_ref, cos_ref, sin_ref, g_ref, b_ref, wg_ref, wu_ref, wd_ref, win_ref,
                h_ref, sbk_ref, sbv_ref, dfk_ref, dfv_ref, sbq_ref, dfq_ref,
                sbkb_ref, sbvb_ref, dfkb_ref, dfvb_ref, *, alpha, width):
    x = x_ref[...]
    ffn = _swiglu(x.astype(_bf16), wg_ref, wu_ref, wd_ref)
    h = _layer_norm(alpha * x + 0.5 * ffn, g_ref[...], b_ref[...])
    h_ref[...] = h
    proj = jnp.dot(h.astype(_bf16), win_ref[...], preferred_element_type=_f32)
    scale = HEAD_DIM ** -0.5 * LOG2E
    sbq_ref[...] = (proj[:, 0:width] * scale).astype(_bf16)
    sbk_ref[...] = proj[:, width:2 * width]
    sbkb_ref[...] = proj[:, width:2 * width].astype(_bf16)
    sbv_ref[...] = proj[:, 2 * width:3 * width]
    sbvb_ref[...] = proj[:, 2 * width:3 * width].astype(_bf16)
    dfv_ref[...] = proj[:, 5 * width:6 * width]
    dfvb_ref[...] = proj[:, 5 * width:6 * width].astype(_bf16)

    cos = cos_ref[...]
    sin = sin_ref[...]
    lane = lax.broadcasted_iota(jnp.int32, cos.shape, 1)
    first_half = (lane % HEAD_DIM) < (HEAD_DIM // 2)

    def rope(xc):
        partner = jnp.where(first_half, pltpu.roll(xc, LANES - HEAD_DIM // 2, 1),
                            pltpu.roll(xc, HEAD_DIM // 2, 1))
        return xc * cos + partner * sin

    for c in range(width // LANES):
        q0 = 3 * width + c * LANES
        k0 = 4 * width + c * LANES
        dfq_ref[:, c * LANES:(c + 1) * LANES] = (rope(proj[:, q0:q0 + LANES]) * scale).astype(_bf16)
        dfk = rope(proj[:, k0:k0 + LANES])
        dfk_ref[:, c * LANES:(c + 1) * LANES] = dfk
        dfkb_ref[:, c * LANES:(c + 1) * LANES] = dfk.astype(_bf16)


def _pre_call(x, cos_t, sin_t, ln_g, ln_b, wg, wu, wd, w_in, *, alpha, tm):
    n, d = x.shape
    width = w_in.shape[1] // 6
    nrep = cos_t.shape[0] // tm
    row = lambda w: pl.BlockSpec((tm, w), lambda i: (i, 0))
    tab = pl.BlockSpec((tm, LANES), lambda i: (i % nrep, 0))
    out_shape = ([jax.ShapeDtypeStruct((n, d), _f32)]
                 + [jax.ShapeDtypeStruct((n, width), _f32)] * 4
                 + [jax.ShapeDtypeStruct((n, width), _bf16)] * 6)
    return pl.pallas_call(
        functools.partial(_pre_kernel, alpha=alpha, width=width),
        grid=(n // tm,),
        in_specs=[row(d), tab, tab, _const_spec(ln_g.shape), _const_spec(ln_b.shape),
                  _const_spec(wg.shape), _const_spec(wu.shape), _const_spec(wd.shape),
                  _const_spec(w_in.shape)],
        out_specs=[row(d)] + [row(width)] * 10,
        out_shape=out_shape,
        compiler_params=pltpu.CompilerParams(dimension_semantics=("arbitrary",),
                                             vmem_limit_bytes=VMEM_LIMIT_BYTES),
        name="pre_ffn_ln_proj",
    )(x, cos_t, sin_t, ln_g, ln_b, wg, wu, wd, w_in)


def _post_kernel(osb_ref, odf_ref, h_ref, wo_sb_ref, wo_df_ref, g2_ref, b2_ref,
                 wg_ref, wu_ref, wd_ref, g3_ref, b3_ref, y_ref, *, alpha):
    mix = (jnp.dot(osb_ref[...], wo_sb_ref[...], preferred_element_type=_f32)
           + jnp.dot(odf_ref[...], wo_df_ref[...], preferred_element_type=_f32))
    x2 = _layer_norm(alpha * h_ref[...] + mix, g2_ref[...], b2_ref[...])
    ffn = _swiglu(x2.astype(_bf16), wg_ref, wu_ref, wd_ref)
    y_ref[...] = _layer_norm(alpha * x2 + 0.5 * ffn, g3_ref[...], b3_ref[...])


def _post_call(o_sb, o_df, h, wo_sb, wo_df, g2, b2, wg, wu, wd, g3, b3, *, alpha, tm):
    n, d = h.shape
    width = o_sb.shape[1]
    row = lambda w: pl.BlockSpec((tm, w), lambda i: (i, 0))
    consts = (wo_sb, wo_df, g2, b2, wg, wu, wd, g3, b3)
    return pl.pallas_call(
        functools.partial(_post_kernel, alpha=alpha),
        grid=(n // tm,),
        in_specs=[row(width), row(width), row(d)] + [_const_spec(c.shape) for c in consts],
        out_specs=row(d),
        out_shape=jax.ShapeDtypeStruct((n, d), _f32),
        compiler_params=pltpu.CompilerParams(dimension_semantics=("arbitrary",),
                                             vmem_limit_bytes=VMEM_LIMIT_BYTES),
        name="post_wo_ln_ffn_ln",
    )(o_sb, o_df, h, *consts)


def _dot_nt(a, b):
    return lax.dot_general(a, b, (((1,), (1,)), ((), ())), preferred_element_type=_f32)


def _stack_halves(q):
    lane = lax.broadcasted_iota(jnp.int32, q.shape, 1)
    zero = jnp.zeros_like(q)
    return jnp.concatenate([jnp.where(lane < HEAD_DIM, q, zero), jnp.where(lane >= HEAD_DIM, q, zero)], axis=0)


def _sb_kernel(q_ref, k_ref, v_ref, o_ref, qs_ref, upper_ref, carry_ref, acc_ref, *, pos0, tq, tk):
    pairs = q_ref.shape[2] // LANES
    p0 = pos0 + pl.program_id(1) * tq
    kb_hi = jnp.maximum(p0 + tq - 2, 0) // tk
    kb_full = p0 // tk - 1
    for h in range(pairs):
        qs_ref[h] = _stack_halves(q_ref[0, :, h * LANES:(h + 1) * LANES])
    upper_ref[...] = (lax.broadcasted_iota(jnp.int32, (tk, tk), 0)
                      > lax.broadcasted_iota(jnp.int32, (tk, tk), 1)).astype(_bf16)
    carry_ref[...] = jnp.zeros(carry_ref.shape, _f32)
    acc_ref[...] = jnp.zeros(acc_ref.shape, _f32)
    col_minus_row = (lax.broadcasted_iota(jnp.int32, (STRIP, LANES), 1)
                     - lax.broadcasted_iota(jnp.int32, (STRIP, LANES), 0))
    strips = range(0, 2 * tq, STRIP)
    lane_blocks = range(0, tk, LANES)

    def tile(kb, masked):
        start = pl.multiple_of(kb * tk, tk)

        def visible(r):
            limit = p0 + (r % tq) - start
            return [col_minus_row < limit - c for c in lane_blocks]

        for h in range(pairs):
            hl = slice(h * LANES, (h + 1) * LANES)
            z = _dot_nt(qs_ref[h], k_ref[0, pl.ds(start, tk), hl])
            lbs, zls, tots = [], [], []
            for r in strips:
                zc = [z[r:r + STRIP, c:c + LANES] for c in lane_blocks]
                zl = [jnp.minimum(x, 0.0) - jnp.log2(1.0 + jnp.exp2(jnp.minimum(x, -x))) for x in zc]
                ls = [a - x for a, x in zip(zl, zc)]
                if masked:
                    ls = [jnp.where(v, l, 0.0) for v, l in zip(visible(r), ls)]
                zls.append(zl)
                lbs.append(jnp.concatenate([l.astype(_bf16) for l in ls], axis=1))
                tots.append(jnp.sum(functools.reduce(jnp.add, ls), axis=1, keepdims=True))
            cum = jnp.dot(jnp.concatenate(lbs, axis=0), upper_ref[...], preferred_element_type=_f32)
            ws = []
            for i, r in enumerate(strips):
                carry = carry_ref[h, r:r + STRIP, :]
                w = [jnp.exp2(zl + cum[r:r + STRIP, c:c + LANES] + carry) for zl, c in zip(zls[i], lane_blocks)]
                if masked:
                    w = [jnp.where(v, x, 0.0) for v, x in zip(visible(r), w)]
                carry_ref[h, r:r + STRIP, :] = carry + tots[i]
                ws.append(jnp.concatenate([x.astype(_bf16) for x in w], axis=1))
            acc_ref[h] += jnp.dot(jnp.concatenate(ws, axis=0), v_ref[0, pl.ds(start, tk), hl],
                                  preferred_element_type=_f32)

    @pl.loop(0, kb_hi - kb_full)
    def _(i):
        tile(kb_hi - i, True)

    def alive(st):
        kb, least_dead = st
        return jnp.logical_and(kb >= 0, least_dead > STICK_DEAD)

    def walk(st):
        kb, _ = st
        tile(kb, False)
        return kb - 1, jnp.max(carry_ref[...])

    lax.while_loop(alive, walk, (kb_full, jnp.max(carry_ref[...])))

    lane = lax.broadcasted_iota(jnp.int32, (tq, LANES), 1)
    for h in range(pairs):
        o_ref[0, :, h * LANES:(h + 1) * LANES] = jnp.where(
            lane < HEAD_DIM, acc_ref[h, 0:tq, :], acc_ref[h, tq:2 * tq, :]).astype(o_ref.dtype)


def _sb_call(q, k, v, *, pos0, tq, tk):
    b, t, width = q.shape
    tkv = k.shape[1]
    pairs = width // LANES
    qspec = pl.BlockSpec((1, tq, width), lambda bi, qi: (bi, qi, 0))
    kvspec = pl.BlockSpec((1, tkv, width), lambda bi, qi: (bi, 0, 0), pipeline_mode=pl.Buffered(1))
    return pl.pallas_call(
        functools.partial(_sb_kernel, pos0=pos0, tq=tq, tk=tk),
        grid=(b, t // tq),
        in_specs=[qspec, kvspec, kvspec],
        out_specs=qspec,
        out_shape=jax.ShapeDtypeStruct((b, t, width), _bf16),
        scratch_shapes=[pltpu.VMEM((pairs, 2 * tq, LANES), _bf16),
                        pltpu.VMEM((tk, tk), _bf16),
                        pltpu.VMEM((pairs, 2 * tq, LANES), _f32),
                        pltpu.VMEM((pairs, 2 * tq, LANES), _f32)],
        compiler_params=pltpu.CompilerParams(
            dimension_semantics=("arbitrary", "arbitrary"),
            vmem_limit_bytes=VMEM_LIMIT_BYTES),
        name="stick_breaking_attention",
    )(q, k, v)


def _diff_kernel(q_ref, k_ref, v_ref, lq1_ref, lk1_ref, lq2_ref, lk2_ref, g_ref, o_ref,
                 qs_ref, m_ref, acc_ref, *, pos0, tq, tk, lambda_init):
    heads = q_ref.shape[2] // LANES
    p0 = pos0 + pl.program_id(1) * tq
    vis_end_first = (p0 // CHUNK + 1) * CHUNK
    vis_end_last = ((p0 + tq - 1) // CHUNK + 1) * CHUNK
    kb_hi = (vis_end_last - 1) // tk
    kb_full = vis_end_first // tk - 1
    for h in range(heads):
        qs_ref[h] = _stack_halves(q_ref[0, :, h * LANES:(h + 1) * LANES])
    m_ref[...] = jnp.full(m_ref.shape, NEG_BIG, _f32)
    acc_ref[...] = jnp.zeros(acc_ref.shape, _f32)
    col = lax.broadcasted_iota(jnp.int32, (STRIP, LANES), 1)
    ones = jnp.ones((tk, LANES), _bf16)

    def tile(kb, masked):
        start = pl.multiple_of(kb * tk, tk)
        for h in range(heads):
            hl = slice(h * LANES, (h + 1) * LANES)
            s = _dot_nt(qs_ref[h], k_ref[0, pl.ds(start, tk), hl])
            probs = []
            for r in range(0, 2 * tq, STRIP):
                rows = slice(r, r + STRIP)
                cols = [s[rows, c:c + LANES] for c in range(0, tk, LANES)]
                if masked:
                    limit = ((r % tq) // CHUNK + 1) * CHUNK + p0 - start
                    vis = [col + c < limit for c in range(0, tk, LANES)]
                    cols = [jnp.where(v, x, NEG_BIG) for v, x in zip(vis, cols)]
                m_old = m_ref[h, rows, :]
                m_new = jnp.maximum(m_old, jnp.max(functools.reduce(jnp.maximum, cols), axis=1, keepdims=True))
                ps = [jnp.exp2(x - m_new) for x in cols]
                if masked:
                    ps = [jnp.where(v, x, 0.0) for v, x in zip(vis, ps)]
                alpha = jnp.exp2(m_old - m_new)
                m_ref[h, rows, :] = m_new
                acc_ref[h, rows, 0:LANES] = alpha * acc_ref[h, rows, 0:LANES]
                acc_ref[h, rows, LANES:2 * LANES] = alpha * acc_ref[h, rows, LANES:2 * LANES]
                probs.append(jnp.concatenate([x.astype(_bf16) for x in ps], axis=1))
            p = jnp.concatenate(probs, axis=0)
            v1 = jnp.concatenate([v_ref[0, pl.ds(start, tk), hl], ones], axis=1)
            acc_ref[h] += jnp.dot(p, v1, preferred_element_type=_f32)

    @pl.loop(0, kb_hi - kb_full)
    def _(i):
        tile(kb_hi - i, True)

    @pl.loop(0, kb_full + 1)
    def _(i):
        tile(kb_full - i, False)

    lam = (jnp.exp(jnp.sum(lq1_ref[...] * lk1_ref[...], axis=1, keepdims=True))
           - jnp.exp(jnp.sum(lq2_ref[...] * lk2_ref[...], axis=1, keepdims=True)) + lambda_init)
    for h in range(heads):
        o = (acc_ref[h, 0:tq, 0:LANES] / acc_ref[h, 0:tq, LANES:2 * LANES]
             - lam * (acc_ref[h, tq:2 * tq, 0:LANES] / acc_ref[h, tq:2 * tq, LANES:2 * LANES]))
        o = o * lax.rsqrt(jnp.mean(o * o, axis=1, keepdims=True) + LN_EPS)
        o_ref[0, :, h * LANES:(h + 1) * LANES] = (o * g_ref[...] * (1.0 - lambda_init)).astype(o_ref.dtype)


def _diff_call(q, k, v, lq1, lk1, lq2, lk2, subln_g, *, pos0, tq, tk, lambda_init):
    b, t, width = q.shape
    tkv = k.shape[1]
    heads = width // LANES
    assert tq % CHUNK == 0 and pos0 % CHUNK == 0 and CHUNK % STRIP == 0
    qspec = pl.BlockSpec((1, tq, width), lambda bi, qi: (bi, qi, 0))
    kvspec = pl.BlockSpec((1, tkv, width), lambda bi, qi: (bi, 0, 0), pipeline_mode=pl.Buffered(1))
    small = (lq1, lk1, lq2, lk2, subln_g)
    return pl.pallas_call(
        functools.partial(_diff_kernel, pos0=pos0, tq=tq, tk=tk, lambda_init=lambda_init),
        grid=(b, t // tq),
        in_specs=[qspec, kvspec, kvspec] + [_const_spec(s.shape) for s in small],
        out_specs=qspec,
        out_shape=jax.ShapeDtypeStruct((b, t, width), _bf16),
        scratch_shapes=[pltpu.VMEM((heads, 2 * tq, LANES), _bf16),
                        pltpu.VMEM((heads, 2 * tq, LANES), _f32),
                        pltpu.VMEM((heads, 2 * tq, 2 * LANES), _f32)],
        compiler_params=pltpu.CompilerParams(
            dimension_semantics=("arbitrary", "arbitrary"),
            vmem_limit_bytes=VMEM_LIMIT_BYTES),
        name="differential_attention",
    )(q, k, v, *small)


def _rope_tables(pos, reps):
    half = HEAD_DIM // 2
    inv_freq = ROPE_THETA ** (-jnp.arange(half, dtype=_f32) / half)
    ang = pos.astype(_f32)[:, None] * inv_freq[None, :]
    cos, sin = jnp.cos(ang), jnp.sin(ang)
    cos_t = jnp.tile(jnp.concatenate([cos, cos], axis=1), (reps, LANES // HEAD_DIM))
    sin_t = jnp.tile(jnp.concatenate([-sin, sin], axis=1), (reps, LANES // HEAD_DIM))
    return cos_t, sin_t


def _layer(x, caches, pos0, p, lambda_init, alpha, *, tm, tq, tq_diff, tk):
    b, t, d = x.shape
    n = b * t
    width = p["w_in"].shape[1] // 6
    tm = min(tm, n)
    if t >= tm:
        cos_t, sin_t = _rope_tables(pos0 + jnp.arange(t), 1)
    else:
        cos_t, sin_t = _rope_tables(pos0 + jnp.arange(t), tm // t)
    h, sbk, sbv, dfk, dfv, sbq, dfq, sbkb, sbvb, dfkb, dfvb = _pre_call(
        x.reshape(n, d), cos_t, sin_t, p["ln1_g"], p["ln1_b"],
        p["ffn1_wg"], p["ffn1_wu"], p["ffn1_wd"], p["w_in"], alpha=alpha, tm=tm)
    rows = (sbk, sbv, dfk, dfv)
    r3 = lambda a: a.reshape(b, t, width)
    if caches is None:
        k_sb, v_sb, k_df, v_df = r3(sbkb), r3(sbvb), r3(dfkb), r3(dfvb)
    else:
        tkv = caches[0].shape[1] + t
        pad = (-tkv) % tk
        cat = lambda c, a: jnp.concatenate(
            [c.reshape(b, -1, width).astype(a.dtype), r3(a), jnp.zeros((b, pad, width), a.dtype)], axis=1)
        k_sb, v_sb, k_df, v_df = (cat(c, a) for c, a in zip(caches, (sbkb, sbvb, dfkb, dfvb)))
    o_sb = _sb_call(r3(sbq), k_sb, v_sb, pos0=pos0, tq=tq, tk=tk)
    o_df = _diff_call(r3(dfq), k_df, v_df, p["lambda_q1"], p["lambda_k1"], p["lambda_q2"],
                      p["lambda_k2"], p["subln_g"], pos0=pos0, tq=tq_diff, tk=tk, lambda_init=lambda_init)
    y = _post_call(o_sb.reshape(n, width), o_df.reshape(n, width), h, p["wo_sb"], p["wo_df"],
                   p["ln2_g"], p["ln2_b"], p["ffn2_wg"], p["ffn2_wu"], p["ffn2_wd"],
                   p["ln3_g"], p["ln3_b"], alpha=alpha, tm=tm)
    return y.reshape(b, t, d), rows


def _run(x_prompt, x_sample, caches, params, *, depth, tm, tq_prompt, tq_prompt_diff, tk):
    assert depth == 1, "the projection / attention layouts assume a single layer"
    alpha = (2 * depth) ** 0.25
    lambda_init = 0.8 - 0.6 * math.exp(-0.3 * 0)
    width = params["w_in"].shape[-1] // 6
    p = {}
    for name, val in params.items():
        val = val[0]
        if name.startswith("ffn") or name == "w_in":
            p[name] = val.astype(_bf16)
        elif name == "w_o":
            p["wo_sb"] = val[:width].astype(_bf16)
            p["wo_df"] = val[width:].astype(_bf16)
        else:
            p[name] = val.reshape(1, -1)
    yp, rp = _layer(x_prompt, None, 0, p, lambda_init, alpha, tm=tm, tq=tq_prompt,
                    tq_diff=min(tq_prompt_diff, x_prompt.shape[1]), tk=tk)
    past = caches[0].shape[2]
    cs = tuple(c[0] for c in caches)
    ys, rs = _layer(x_sample, cs, past, p, lambda_init, alpha, tm=tm, tq=x_sample.shape[1],
                    tq_diff=x_sample.shape[1], tk=tk)
    return yp, ys, rp, rs


def kernel(x_prompt, x_sample, cache_sb_k, cache_sb_v, cache_diff_k, cache_diff_v, ln1_g, ln1_b, ffn1_wg, ffn1_wu, ffn1_wd, w_in, lambda_q1, lambda_k1, lambda_q2, lambda_k2, subln_g, w_o, ln2_g, ln2_b, ffn2_wg, ffn2_wu, ffn2_wd, ln3_g, ln3_b):
    params = dict(ln1_g=ln1_g, ln1_b=ln1_b, ffn1_wg=ffn1_wg, ffn1_wu=ffn1_wu, ffn1_wd=ffn1_wd,
                  w_in=w_in, lambda_q1=lambda_q1, lambda_k1=lambda_k1, lambda_q2=lambda_q2,
                  lambda_k2=lambda_k2, subln_g=subln_g, w_o=w_o, ln2_g=ln2_g, ln2_b=ln2_b,
                  ffn2_wg=ffn2_wg, ffn2_wu=ffn2_wu, ffn2_wd=ffn2_wd, ln3_g=ln3_g, ln3_b=ln3_b)
    depth = w_in.shape[0]
    yp, ys, rp, rs = _run(x_prompt, x_sample, (cache_sb_k, cache_sb_v, cache_diff_k, cache_diff_v),
                          params, depth=depth, tm=512, tq_prompt=256, tq_prompt_diff=512, tk=256)
    bp, tp = x_prompt.shape[:2]
    bs, ts = x_sample.shape[:2]
    heads = rp[0].shape[-1] // HEAD_DIM

    def shapes(rows, b, t):
        sbk, sbv, dfk, dfv = rows
        return (sbk.reshape(depth, b, t, heads, HEAD_DIM), sbv.reshape(depth, b, t, heads, HEAD_DIM),
                dfk.reshape(depth, b, t, heads // 2, 2, HEAD_DIM),
                dfv.reshape(depth, b, t, heads // 2, 2 * HEAD_DIM))

    return (yp, ys) + shapes(rp, bp, tp) + shapes(rs, bs, ts)
```

```python
import functools
import math

import jax
import jax.numpy as jnp
from jax import lax
from jax.experimental import pallas as pl
from jax.experimental.pallas import tpu as pltpu

HEAD_DIM = 64
CHUNK = 64
LANES = 128
STRIP = 16
ROPE_THETA = 10000.0
LN_EPS = 1e-5
VMEM_LIMIT_BYTES = 56 * 1024 * 1024
STICK_DEAD = -160.0
NEG_BIG = -1e30
LOG2E = math.log2(math.e)

_f32 = jnp.float32
_bf16 = jnp.bfloat16


def _layer_norm(x, g, b):
    mu = jnp.mean(x, axis=-1, keepdims=True)
    xc = x - mu
    var = jnp.mean(xc * xc, axis=-1, keepdims=True)
    return xc * lax.rsqrt(var + LN_EPS) * g + b


def _swiglu(xb, wg_ref, wu_ref, wd_ref):
    gate = jnp.dot(xb, wg_ref[...], preferred_element_type=_f32)
    up = jnp.dot(xb, wu_ref[...], preferred_element_type=_f32)
    act = (gate * jax.nn.sigmoid(gate) * up).astype(_bf16)
    return jnp.dot(act, wd_ref[...], preferred_element_type=_f32)


def _const_spec(shape):
    nd = len(shape)
    return pl.BlockSpec(shape, lambda *_: (0,) * nd, pipeline_mode=pl.Buffered(1))


def _store_heads(ref, x):
    tm, w = x.shape
    heads = w // HEAD_DIM
    for hd in range(heads):
        c0 = (hd // 2) * LANES
        blk = x[:, c0:c0 + LANES]
        if hd % 2:
            blk = pltpu.roll(blk, HEAD_DIM, 1)
        ref[pl.ds(hd, tm, stride=heads), :] = blk[:, 0:HEAD_DIM]


def _pre_kernel(x_ref, cos_ref, sin_ref, g_ref, b_ref, wg_ref, wu_ref, wd_ref, win_ref,
                h_ref, sbk_ref, sbv_ref, dfk_ref, dfv_ref, sbq_ref, dfq_ref,
                sbkb_ref, sbvb_ref, dfkb_ref, dfvb_ref, *, alpha, width):
    x = x_ref[...]
    ffn = _swiglu(x.astype(_bf16), wg_ref, wu_ref, wd_ref)
    h = _layer_norm(alpha * x + 0.5 * ffn, g_ref[...], b_ref[...])
    h_ref[...] = h
    proj = jnp.dot(h.astype(_bf16), win_ref[...], preferred_element_type=_f32)
    scale = HEAD_DIM ** -0.5 * LOG2E
    sbq_ref[...] = (proj[:, 0:width] * scale).astype(_bf16)
    _store_heads(sbk_ref, proj[:, width:2 * width])
    sbkb_ref[...] = proj[:, width:2 * width].astype(_bf16)
    _store_heads(sbv_ref, proj[:, 2 * width:3 * width])
    sbvb_ref[...] = proj[:, 2 * width:3 * width].astype(_bf16)
    dfv_ref[...] = proj[:, 5 * width:6 * width]
    dfvb_ref[...] = proj[:, 5 * width:6 * width].astype(_bf16)

    cos = cos_ref[...]
    sin = sin_ref[...]
    lane = lax.broadcasted_iota(jnp.int32, cos.shape, 1)
    first_half = (lane % HEAD_DIM) < (HEAD_DIM // 2)

    def rope(xc):
        partner = jnp.where(first_half, pltpu.roll(xc, LANES - HEAD_DIM // 2, 1),
                            pltpu.roll(xc, HEAD_DIM // 2, 1))
        return xc * cos + partner * sin

    for c in range(width // LANES):
        q0 = 3 * width + c * LANES
        k0 = 4 * width + c * LANES
        dfq_ref[:, c * LANES:(c + 1) * LANES] = (rope(proj[:, q0:q0 + LANES]) * scale).astype(_bf16)
        dfk = rope(proj[:, k0:k0 + LANES])
        dfk_ref[:, c * LANES:(c + 1) * LANES] = dfk
        dfkb_ref[:, c * LANES:(c + 1) * LANES] = dfk.astype(_bf16)


def _pre_call(x, cos_t, sin_t, ln_g, ln_b, wg, wu, wd, w_in, *, alpha, tm):
    n, d = x.shape
    width = w_in.shape[1] // 6
    nrep = cos_t.shape[0] // tm
    row = lambda w: pl.BlockSpec((tm, w), lambda i: (i, 0))
    tab = pl.BlockSpec((tm, LANES), lambda i: (i % nrep, 0))
    heads = width // HEAD_DIM
    out_shape = ([jax.ShapeDtypeStruct((n, d), _f32)]
                 + [jax.ShapeDtypeStruct((n * heads, HEAD_DIM), _f32)] * 2
                 + [jax.ShapeDtypeStruct((n, width), _f32)] * 2
                 + [jax.ShapeDtypeStruct((n, width), _bf16)] * 6)
    return pl.pallas_call(
        functools.partial(_pre_kernel, alpha=alpha, width=width),
        grid=(n // tm,),
        in_specs=[row(d), tab, tab, _const_spec(ln_g.shape), _const_spec(ln_b.shape),
                  _const_spec(wg.shape), _const_spec(wu.shape), _const_spec(wd.shape),
                  _const_spec(w_in.shape)],
        out_specs=([row(d)] + [pl.BlockSpec((tm * heads, HEAD_DIM), lambda i: (i, 0))] * 2
                   + [row(width)] * 8),
        out_shape=out_shape,
        compiler_params=pltpu.CompilerParams(dimension_semantics=("arbitrary",),
                                             vmem_limit_bytes=VMEM_LIMIT_BYTES),
        name="pre_ffn_ln_proj",
    )(x, cos_t, sin_t, ln_g, ln_b, wg, wu, wd, w_in)


def _post_kernel(osb_ref, odf_ref, h_ref, wo_sb_ref, wo_df_ref, g2_ref, b2_ref,
                 wg_ref, wu_ref, wd_ref, g3_ref, b3_ref, y_ref, *, alpha):
    mix = (jnp.dot(osb_ref[...], wo_sb_ref[...], preferred_element_type=_f32)
           + jnp.dot(odf_ref[...], wo_df_ref[...], preferred_element_type=_f32))
    x2 = _layer_norm(alpha * h_ref[...] + mix, g2_ref[...], b2_ref[...])
    ffn = _swiglu(x2.astype(_bf16), wg_ref, wu_ref, wd_ref)
    y_ref[...] = _layer_norm(alpha * x2 + 0.5 * ffn, g3_ref[...], b3_ref[...])


def _post_call(o_sb, o_df, h, wo_sb, wo_df, g2, b2, wg, wu, wd, g3, b3, *, alpha, tm):
    n, d = h.shape
    width = o_sb.shape[1]
    row = lambda w: pl.BlockSpec((tm, w), lambda i: (i, 0))
    consts = (wo_sb, wo_df, g2, b2, wg, wu, wd, g3, b3)
    return pl.pallas_call(
        functools.partial(_post_kernel, alpha=alpha),
        grid=(n // tm,),
        in_specs=[row(width), row(width), row(d)] + [_const_spec(c.shape) for c in consts],
        out_specs=row(d),
        out_shape=jax.ShapeDtypeStruct((n, d), _f32),
        compiler_params=pltpu.CompilerParams(dimension_semantics=("arbitrary",),
                                             vmem_limit_bytes=VMEM_LIMIT_BYTES),
        name="post_wo_ln_ffn_ln",
    )(o_sb, o_df, h, *consts)


def _dot_nt(a, b):
    return lax.dot_general(a, b, (((1,), (1,)), ((), ())), preferred_element_type=_f32)


def _stack_halves(q):
    lane = lax.broadcasted_iota(jnp.int32, q.shape, 1)
    zero = jnp.zeros_like(q)
    return jnp.concatenate([jnp.where(lane < HEAD_DIM, q, zero), jnp.where(lane >= HEAD_DIM, q, zero)], axis=0)


def _sb_kernel(q_ref, k_ref, v_ref, o_ref, qs_ref, upper_ref, carry_ref, acc_ref, *, pos0, tq, tk):
    pairs = q_ref.shape[2] // LANES
    p0 = pos0 + pl.program_id(1) * tq
    kb_hi = jnp.maximum(p0 + tq - 2, 0) // tk
    kb_full = p0 // tk - 1
    for h in range(pairs):
        qs_ref[h] = _stack_halves(q_ref[0, :, h * LANES:(h + 1) * LANES])
    upper_ref[...] = (lax.broadcasted_iota(jnp.int32, (tk, tk), 0)
                      > lax.broadcasted_iota(jnp.int32, (tk, tk), 1)).astype(_bf16)
    carry_ref[...] = jnp.zeros(carry_ref.shape, _f32)
    acc_ref[...] = jnp.zeros(acc_ref.shape, _f32)
    col_minus_row = (lax.broadcasted_iota(jnp.int32, (STRIP, LANES), 1)
                     - lax.broadcasted_iota(jnp.int32, (STRIP, LANES), 0))
    strips = range(0, 2 * tq, STRIP)
    lane_blocks = range(0, tk, LANES)

    def tile(kb, masked):
        start = pl.multiple_of(kb * tk, tk)

        def visible(r):
            limit = p0 + (r % tq) - start
            return [col_minus_row < limit - c for c in lane_blocks]

        for h in range(pairs):
            hl = slice(h * LANES, (h + 1) * LANES)
            z = _dot_nt(qs_ref[h], k_ref[0, pl.ds(start, tk), hl])
            lbs, zls, tots = [], [], []
            for r in strips:
                zc = [z[r:r + STRIP, c:c + LANES] for c in lane_blocks]
                zl = [jnp.minimum(x, 0.0) - jnp.log2(1.0 + jnp.exp2(jnp.minimum(x, -x))) for x in zc]
                ls = [a - x for a, x in zip(zl, zc)]
                if masked:
                    ls = [jnp.where(v, l, 0.0) for v, l in zip(visible(r), ls)]
                zls.append(zl)
                lbs.append(jnp.concatenate([l.astype(_bf16) for l in ls], axis=1))
                tots.append(jnp.sum(functools.reduce(jnp.add, ls), axis=1, keepdims=True))
            cum = jnp.dot(jnp.concatenate(lbs, axis=0), upper_ref[...], preferred_element_type=_f32)
            ws = []
            for i, r in enumerate(strips):
                carry = carry_ref[h, r:r + STRIP, :]
                w = [jnp.exp2(zl + cum[r:r + STRIP, c:c + LANES] + carry) for zl, c in zip(zls[i], lane_blocks)]
                if masked:
                    w = [jnp.where(v, x, 0.0) for v, x in zip(visible(r), w)]
                carry_ref[h, r:r + STRIP, :] = carry + tots[i]
                ws.append(jnp.concatenate([x.astype(_bf16) for x in w], axis=1))
            acc_ref[h] += jnp.dot(jnp.concatenate(ws, axis=0), v_ref[0, pl.ds(start, tk), hl],
                                  preferred_element_type=_f32)

    @pl.loop(0, kb_hi - kb_full)
    def _(i):
        tile(kb_hi - i, True)

    def alive(st):
        kb, least_dead = st
        return jnp.logical_and(kb >= 0, least_dead > STICK_DEAD)

    def walk(st):
        kb, _ = st
        tile(kb, False)
        return kb - 1, jnp.max(carry_ref[...])

    lax.while_loop(alive, walk, (kb_full, jnp.max(carry_ref[...])))

    lane = lax.broadcasted_iota(jnp.int32, (tq, LANES), 1)
    for h in range(pairs):
        o_ref[0, :, h * LANES:(h + 1) * LANES] = jnp.where(
            lane < HEAD_DIM, acc_ref[h, 0:tq, :], acc_ref[h, tq:2 * tq, :]).astype(o_ref.dtype)


def _sb_call(q, k, v, *, pos0, tq, tk):
    b, t, width = q.shape
    tkv = k.shape[1]
    pairs = width // LANES
    qspec = pl.BlockSpec((1, tq, width), lambda bi, qi: (bi, qi, 0))
    kvspec = pl.BlockSpec((1, tkv, width), lambda bi, qi: (bi, 0, 0), pipeline_mode=pl.Buffered(1))
    return pl.pallas_call(
        functools.partial(_sb_kernel, pos0=pos0, tq=tq, tk=tk),
        grid=(b, t // tq),
        in_specs=[qspec, kvspec, kvspec],
        out_specs=qspec,
        out_shape=jax.ShapeDtypeStruct((b, t, width), _bf16),
        scratch_shapes=[pltpu.VMEM((pairs, 2 * tq, LANES), _bf16),
                        pltpu.VMEM((tk, tk), _bf16),
                        pltpu.VMEM((pairs, 2 * tq, LANES), _f32),
                        pltpu.VMEM((pairs, 2 * tq, LANES), _f32)],
        compiler_params=pltpu.CompilerParams(
            dimension_semantics=("arbitrary", "arbitrary"),
            vmem_limit_bytes=VMEM_LIMIT_BYTES),
        name="stick_breaking_attention",
    )(q, k, v)


def _diff_kernel(q_ref, k_ref, v_ref, lq1_ref, lk1_ref, lq2_ref, lk2_ref, g_ref, o_ref,
                 qs_ref, m_ref, acc_ref, *, pos0, tq, tk, lambda_init):
    heads = q_ref.shape[2] // LANES
    p0 = pos0 + pl.program_id(1) * tq
    vis_end_first = (p0 // CHUNK + 1) * CHUNK
    vis_end_last = ((p0 + tq - 1) // CHUNK + 1) * CHUNK
    kb_hi = (vis_end_last - 1) // tk
    kb_full = vis_end_first // tk - 1
    for h in range(heads):
        qs_ref[h] = _stack_halves(q_ref[0, :, h * LANES:(h + 1) * LANES])
    m_ref[...] = jnp.full(m_ref.shape, NEG_BIG, _f32)
    acc_ref[...] = jnp.zeros(acc_ref.shape, _f32)
    col = lax.broadcasted_iota(jnp.int32, (STRIP, LANES), 1)
    ones = jnp.ones((tk, LANES), _bf16)

    def tile(kb, masked):
        start = pl.multiple_of(kb * tk, tk)
        for h in range(heads):
            hl = slice(h * LANES, (h + 1) * LANES)
            s = _dot_nt(qs_ref[h], k_ref[0, pl.ds(start, tk), hl])
            probs = []
            for r in range(0, 2 * tq, STRIP):
                rows = slice(r, r + STRIP)
                cols = [s[rows, c:c + LANES] for c in range(0, tk, LANES)]
                if masked:
                    limit = ((r % tq) // CHUNK + 1) * CHUNK + p0 - start
                    vis = [col + c < limit for c in range(0, tk, LANES)]
                    cols = [jnp.where(v, x, NEG_BIG) for v, x in zip(vis, cols)]
                m_old = m_ref[h, rows, :]
                m_new = jnp.maximum(m_old, jnp.max(functools.reduce(jnp.maximum, cols), axis=1, keepdims=True))
                ps = [jnp.exp2(x - m_new) for x in cols]
                if masked:
                    ps = [jnp.where(v, x, 0.0) for v, x in zip(vis, ps)]
                alpha = jnp.exp2(m_old - m_new)
                m_ref[h, rows, :] = m_new
                acc_ref[h, rows, 0:LANES] = alpha * acc_ref[h, rows, 0:LANES]
                acc_ref[h, rows, LANES:2 * LANES] = alpha * acc_ref[h, rows, LANES:2 * LANES]
                probs.append(jnp.concatenate([x.astype(_bf16) for x in ps], axis=1))
            p = jnp.concatenate(probs, axis=0)
            v1 = jnp.concatenate([v_ref[0, pl.ds(start, tk), hl], ones], axis=1)
            acc_ref[h] += jnp.dot(p, v1, preferred_element_type=_f32)

    @pl.loop(0, kb_hi - kb_full)
    def _(i):
        tile(kb_hi - i, True)

    @pl.loop(0, kb_full + 1)
    def _(i):
        tile(kb_full - i, False)

    lam = (jnp.exp(jnp.sum(lq1_ref[...] * lk1_ref[...], axis=1, keepdims=True))
           - jnp.exp(jnp.sum(lq2_ref[...] * lk2_ref[...], axis=1, keepdims=True)) + lambda_init)
    for h in range(heads):
        o = (acc_ref[h, 0:tq, 0:LANES] / acc_ref[h, 0:tq, LANES:2 * LANES]
             - lam * (acc_ref[h, tq:2 * tq, 0:LANES] / acc_ref[h, tq:2 * tq, LANES:2 * LANES]))
        o = o * lax.rsqrt(jnp.mean(o * o, axis=1, keepdims=True) + LN_EPS)
        o_ref[0, :, h * LANES:(h + 1) * LANES] = (o * g_ref[...] * (1.0 - lambda_init)).astype(o_ref.dtype)


def _diff_call(q, k, v, lq1, lk1, lq2, lk2, subln_g, *, pos0, tq, tk, lambda_init):
    b, t, width = q.shape
    tkv = k.shape[1]
    heads = width // LANES
    assert tq % CHUNK == 0 and pos0 % CHUNK == 0 and CHUNK % STRIP == 0
    qspec = pl.BlockSpec((1, tq, width), lambda bi, qi: (bi, qi, 0))
    kvspec = pl.BlockSpec((1, tkv, width), lambda bi, qi: (bi, 0, 0), pipeline_mode=pl.Buffered(1))
    small = (lq1, lk1, lq2, lk2, subln_g)
    return pl.pallas_call(
        functools.partial(_diff_kernel, pos0=pos0, tq=tq, tk=tk, lambda_init=lambda_init),
        grid=(b, t // tq),
        in_specs=[qspec, kvspec, kvspec] + [_const_spec(s.shape) for s in small],
        out_specs=qspec,
        out_shape=jax.ShapeDtypeStruct((b, t, width), _bf16),
        scratch_shapes=[pltpu.VMEM((heads, 2 * tq, LANES), _bf16),
                        pltpu.VMEM((heads, 2 * tq, LANES), _f32),
                        pltpu.VMEM((heads, 2 * tq, 2 * LANES), _f32)],
        compiler_params=pltpu.CompilerParams(
            dimension_semantics=("arbitrary", "arbitrary"),
            vmem_limit_bytes=VMEM_LIMIT_BYTES),
        name="differential_attention",
    )(q, k, v, *small)


def _rope_tables(pos, reps):
    half = HEAD_DIM // 2
    inv_freq = ROPE_THETA ** (-jnp.arange(half, dtype=_f32) / half)
    ang = pos.astype(_f32)[:, None] * inv_freq[None, :]
    cos, sin = jnp.cos(ang), jnp.sin(ang)
    cos_t = jnp.tile(jnp.concatenate([cos, cos], axis=1), (reps, LANES // HEAD_DIM))
    sin_t = jnp.tile(jnp.concatenate([-sin, sin], axis=1), (reps, LANES // HEAD_DIM))
    return cos_t, sin_t


def _layer(x, caches, pos0, p, lambda_init, alpha, *, tm, tq, tq_diff, tk):
    b, t, d = x.shape
    n = b * t
    width = p["w_in"].shape[1] // 6
    tm = min(tm, n)
    if t >= tm:
        cos_t, sin_t = _rope_tables(pos0 + jnp.arange(t), 1)
    else:
        cos_t, sin_t = _rope_tables(pos0 + jnp.arange(t), tm // t)
    h, sbk, sbv, dfk, dfv, sbq, dfq, sbkb, sbvb, dfkb, dfvb = _pre_call(
        x.reshape(n, d), cos_t, sin_t, p["ln1_g"], p["ln1_b"],
        p["ffn1_wg"], p["ffn1_wu"], p["ffn1_wd"], p["w_in"], alpha=alpha, tm=tm // 2)
    rows = (sbk, sbv, dfk, dfv)
    r3 = lambda a: a.reshape(b, t, width)
    if caches is None:
        k_sb, v_sb, k_df, v_df = r3(sbkb), r3(sbvb), r3(dfkb), r3(dfvb)
    else:
        tkv = caches[0].shape[1] + t
        pad = (-tkv) % tk
        cat = lambda c, a: jnp.concatenate(
            [c.reshape(b, -1, width).astype(a.dtype), r3(a), jnp.zeros((b, pad, width), a.dtype)], axis=1)
        k_sb, v_sb, k_df, v_df = (cat(c, a) for c, a in zip(caches, (sbkb, sbvb, dfkb, dfvb)))
    o_sb = _sb_call(r3(sbq), k_sb, v_sb, pos0=pos0, tq=tq, tk=tk)
    o_df = _diff_call(r3(dfq), k_df, v_df, p["lambda_q1"], p["lambda_k1"], p["lambda_q2"],
                      p["lambda_k2"], p["subln_g"], pos0=pos0, tq=tq_diff, tk=tk, lambda_init=lambda_init)
    y = _post_call(o_sb.reshape(n, width), o_df.reshape(n, width), h, p["wo_sb"], p["wo_df"],
                   p["ln2_g"], p["ln2_b"], p["ffn2_wg"], p["ffn2_wu"], p["ffn2_wd"],
                   p["ln3_g"], p["ln3_b"], alpha=alpha, tm=tm)
    return y.reshape(b, t, d), rows


def _run(x_prompt, x_sample, caches, params, *, depth, tm, tq_prompt, tq_prompt_diff, tk):
    assert depth == 1, "the projection / attention layouts assume a single layer"
    alpha = (2 * depth) ** 0.25
    lambda_init = 0.8 - 0.6 * math.exp(-0.3 * 0)
    width = params["w_in"].shape[-1] // 6
    p = {}
    for name, val in params.items():
        val = val[0]
        if name.startswith("ffn") or name == "w_in":
            p[name] = val.astype(_bf16)
        elif name == "w_o":
            p["wo_sb"] = val[:width].astype(_bf16)
            p["wo_df"] = val[width:].astype(_bf16)
        else:
            p[name] = val.reshape(1, -1)
    yp, rp = _layer(x_prompt, None, 0, p, lambda_init, alpha, tm=tm, tq=tq_prompt,
                    tq_diff=min(tq_prompt_diff, x_prompt.shape[1]), tk=tk)
    past = caches[0].shape[2]
    cs = tuple(c[0] for c in caches)
    ys, rs = _layer(x_sample, cs, past, p, lambda_init, alpha, tm=tm, tq=x_sample.shape[1],
                    tq_diff=x_sample.shape[1], tk=tk)
    return yp, ys, rp, rs


def kernel(x_prompt, x_sample, cache_sb_k, cache_sb_v, cache_diff_k, cache_diff_v, ln1_g, ln1_b, ffn1_wg, ffn1_wu, ffn1_wd, w_in, lambda_q1, lambda_k1, lambda_q2, lambda_k2, subln_g, w_o, ln2_g, ln2_b, ffn2_wg, ffn2_wu, ffn2_wd, ln3_g, ln3_b):
    params = dict(ln1_g=ln1_g, ln1_b=ln1_b, ffn1_wg=ffn1_wg, ffn1_wu=ffn1_wu, ffn1_wd=ffn1_wd,
                  w_in=w_in, lambda_q1=lambda_q1, lambda_k1=lambda_k1, lambda_q2=lambda_q2,
                  lambda_k2=lambda_k2, subln_g=subln_g, w_o=w_o, ln2_g=ln2_g, ln2_b=ln2_b,
                  ffn2_wg=ffn2_wg, ffn2_wu=ffn2_wu, ffn2_wd=ffn2_wd, ln3_g=ln3_g, ln3_b=ln3_b)
    depth = w_in.shape[0]
    yp, ys, rp, rs = _run(x_prompt, x_sample, (cache_sb_k, cache_sb_v, cache_diff_k, cache_diff_v),
                          params, depth=depth, tm=512, tq_prompt=256, tq_prompt_diff=512, tk=256)
    bp, tp = x_prompt.shape[:2]
    bs, ts = x_sample.shape[:2]
    heads = rp[2].shape[-1] // HEAD_DIM

    def shapes(rows, b, t):
        sbk, sbv, dfk, dfv = rows
        return (sbk.reshape(depth, b, t, heads, HEAD_DIM), sbv.reshape(depth, b, t, heads, HEAD_DIM),
                dfk.reshape(depth, b, t, heads // 2, 2, HEAD_DIM),
                dfv.reshape(depth, b, t, heads // 2, 2 * HEAD_DIM))

    return (yp, ys) + shapes(rp, bp, tp) + shapes(rs, bs, ts)
```

```python
import functools
import math

import jax
import jax.numpy as jnp
from jax import lax
from jax.experimental import pallas as pl
from jax.experimental.pallas import tpu as pltpu

HEAD_DIM = 64
CHUNK = 64
LANES = 128
STRIP = 16
ROPE_THETA = 10000.0
LN_EPS = 1e-5
VMEM_LIMIT_BYTES = 56 * 1024 * 1024
STICK_DEAD = -160.0
NEG_BIG = -1e30
LOG2E = math.log2(math.e)

_f32 = jnp.float32
_bf16 = jnp.bfloat16


def _layer_norm(x, g, b):
    mu = jnp.mean(x, axis=-1, keepdims=True)
    xc = x - mu
    var = jnp.mean(xc * xc, axis=-1, keepdims=True)
    return xc * lax.rsqrt(var + LN_EPS) * g + b


def _swiglu(xb, wg_ref, wu_ref, wd_ref):
    gate = jnp.dot(xb, wg_ref[...], preferred_element_type=_f32)
    up = jnp.dot(xb, wu_ref[...], preferred_element_type=_f32)
    act = (gate * jax.nn.sigmoid(gate) * up).astype(_bf16)
    return jnp.dot(act, wd_ref[...], preferred_element_type=_f32)


def _const_spec(shape):
    nd = len(shape)
    return pl.BlockSpec(shape, lambda *_: (0,) * nd, pipeline_mode=pl.Buffered(1))


def _store_heads(ref, x, head_width=HEAD_DIM):
    tm, w = x.shape
    heads = w // head_width
    for hd in range(heads):
        c0 = (hd * head_width) // LANES * LANES
        blk = x[:, c0:c0 + LANES]
        off = hd * head_width - c0
        if off:
            blk = pltpu.roll(blk, LANES - off, 1)
        ref[pl.ds(hd, tm, stride=heads), :] = blk[:, 0:head_width]


def _pre_kernel(x_ref, cos_ref, sin_ref, g_ref, b_ref, wg_ref, wu_ref, wd_ref, win_ref,
                h_ref, sbk_ref, sbv_ref, dfk_ref, dfv_ref, sbq_ref, dfq_ref,
                sbkb_ref, sbvb_ref, dfkb_ref, dfvb_ref, *, alpha, width):
    x = x_ref[...]
    ffn = _swiglu(x.astype(_bf16), wg_ref, wu_ref, wd_ref)
    h = _layer_norm(alpha * x + 0.5 * ffn, g_ref[...], b_ref[...])
    h_ref[...] = h
    proj = jnp.dot(h.astype(_bf16), win_ref[...], preferred_element_type=_f32)
    scale = HEAD_DIM ** -0.5 * LOG2E
    sbq_ref[...] = (proj[:, 0:width] * scale).astype(_bf16)
    _store_heads(sbk_ref, proj[:, width:2 * width])
    sbkb_ref[...] = proj[:, width:2 * width].astype(_bf16)
    _store_heads(sbv_ref, proj[:, 2 * width:3 * width])
    sbvb_ref[...] = proj[:, 2 * width:3 * width].astype(_bf16)
    _store_heads(dfv_ref, proj[:, 5 * width:6 * width], 2 * HEAD_DIM)
    dfvb_ref[...] = proj[:, 5 * width:6 * width].astype(_bf16)

    cos = cos_ref[...]
    sin = sin_ref[...]
    lane = lax.broadcasted_iota(jnp.int32, cos.shape, 1)
    first_half = (lane % HEAD_DIM) < (HEAD_DIM // 2)

    def rope(xc):
        partner = jnp.where(first_half, pltpu.roll(xc, LANES - HEAD_DIM // 2, 1),
                            pltpu.roll(xc, HEAD_DIM // 2, 1))
        return xc * cos + partner * sin

    for c in range(width // LANES):
        q0 = 3 * width + c * LANES
        k0 = 4 * width + c * LANES
        dfq_ref[:, c * LANES:(c + 1) * LANES] = (rope(proj[:, q0:q0 + LANES]) * scale).astype(_bf16)
        dfk = rope(proj[:, k0:k0 + LANES])
        dfk_ref[pl.ds(2 * c, dfk.shape[0], stride=2 * (width // LANES)), :] = dfk[:, 0:HEAD_DIM]
        dfk_ref[pl.ds(2 * c + 1, dfk.shape[0], stride=2 * (width // LANES)), :] = (
            pltpu.roll(dfk, HEAD_DIM, 1)[:, 0:HEAD_DIM])
        dfkb_ref[:, c * LANES:(c + 1) * LANES] = dfk.astype(_bf16)


def _pre_call(x, cos_t, sin_t, ln_g, ln_b, wg, wu, wd, w_in, *, alpha, tm):
    n, d = x.shape
    width = w_in.shape[1] // 6
    nrep = cos_t.shape[0] // tm
    row = lambda w: pl.BlockSpec((tm, w), lambda i: (i, 0))
    tab = pl.BlockSpec((tm, LANES), lambda i: (i % nrep, 0))
    heads = width // HEAD_DIM
    out_shape = ([jax.ShapeDtypeStruct((n, d), _f32)]
                 + [jax.ShapeDtypeStruct((n * heads, HEAD_DIM), _f32)] * 3
                 + [jax.ShapeDtypeStruct((n * heads // 2, 2 * HEAD_DIM), _f32)]
                 + [jax.ShapeDtypeStruct((n, width), _bf16)] * 6)
    return pl.pallas_call(
        functools.partial(_pre_kernel, alpha=alpha, width=width),
        grid=(n // tm,),
        in_specs=[row(d), tab, tab, _const_spec(ln_g.shape), _const_spec(ln_b.shape),
                  _const_spec(wg.shape), _const_spec(wu.shape), _const_spec(wd.shape),
                  _const_spec(w_in.shape)],
        out_specs=([row(d)] + [pl.BlockSpec((tm * heads, HEAD_DIM), lambda i: (i, 0))] * 3
                   + [pl.BlockSpec((tm * heads // 2, 2 * HEAD_DIM), lambda i: (i, 0))]
                   + [row(width)] * 6),
        out_shape=out_shape,
        compiler_params=pltpu.CompilerParams(dimension_semantics=("arbitrary",),
                                             vmem_limit_bytes=VMEM_LIMIT_BYTES),
        name="pre_ffn_ln_proj",
    )(x, cos_t, sin_t, ln_g, ln_b, wg, wu, wd, w_in)


def _post_kernel(osb_ref, odf_ref, h_ref, wo_sb_ref, wo_df_ref, g2_ref, b2_ref,
                 wg_ref, wu_ref, wd_ref, g3_ref, b3_ref, y_ref, *, alpha):
    mix = (jnp.dot(osb_ref[...], wo_sb_ref[...], preferred_element_type=_f32)
           + jnp.dot(odf_ref[...], wo_df_ref[...], preferred_element_type=_f32))
    x2 = _layer_norm(alpha * h_ref[...] + mix, g2_ref[...], b2_ref[...])
    ffn = _swiglu(x2.astype(_bf16), wg_ref, wu_ref, wd_ref)
    y_ref[...] = _layer_norm(alpha * x2 + 0.5 * ffn, g3_ref[...], b3_ref[...])


def _post_call(o_sb, o_df, h, wo_sb, wo_df, g2, b2, wg, wu, wd, g3, b3, *, alpha, tm):
    n, d = h.shape
    width = o_sb.shape[1]
    row = lambda w: pl.BlockSpec((tm, w), lambda i: (i, 0))
    consts = (wo_sb, wo_df, g2, b2, wg, wu, wd, g3, b3)
    return pl.pallas_call(
        functools.partial(_post_kernel, alpha=alpha),
        grid=(n // tm,),
        in_specs=[row(width), row(width), row(d)] + [_const_spec(c.shape) for c in consts],
        out_specs=row(d),
        out_shape=jax.ShapeDtypeStruct((n, d), _f32),
        compiler_params=pltpu.CompilerParams(dimension_semantics=("arbitrary",),
                                             vmem_limit_bytes=VMEM_LIMIT_BYTES),
        name="post_wo_ln_ffn_ln",
    )(o_sb, o_df, h, *consts)


def _dot_nt(a, b):
    return lax.dot_general(a, b, (((1,), (1,)), ((), ())), preferred_element_type=_f32)


def _stack_halves(q):
    lane = lax.broadcasted_iota(jnp.int32, q.shape, 1)
    zero = jnp.zeros_like(q)
    return jnp.concatenate([jnp.where(lane < HEAD_DIM, q, zero), jnp.where(lane >= HEAD_DIM, q, zero)], axis=0)


def _sb_kernel(q_ref, k_ref, v_ref, o_ref, qs_ref, upper_ref, carry_ref, acc_ref, *, pos0, tq, tk):
    pairs = q_ref.shape[2] // LANES
    p0 = pos0 + pl.program_id(1) * tq
    kb_hi = jnp.maximum(p0 + tq - 2, 0) // tk
    kb_full = p0 // tk - 1
    for h in range(pairs):
        qs_ref[h] = _stack_halves(q_ref[0, :, h * LANES:(h + 1) * LANES])
    upper_ref[...] = (lax.broadcasted_iota(jnp.int32, (tk, tk), 0)
                      > lax.broadcasted_iota(jnp.int32, (tk, tk), 1)).astype(_bf16)
    carry_ref[...] = jnp.zeros(carry_ref.shape, _f32)
    acc_ref[...] = jnp.zeros(acc_ref.shape, _f32)
    col_minus_row = (lax.broadcasted_iota(jnp.int32, (STRIP, LANES), 1)
                     - lax.broadcasted_iota(jnp.int32, (STRIP, LANES), 0))
    strips = range(0, 2 * tq, STRIP)
    lane_blocks = range(0, tk, LANES)

    def tile(kb, masked):
        start = pl.multiple_of(kb * tk, tk)

        def visible(r):
            limit = p0 + (r % tq) - start
            return [col_minus_row < limit - c for c in lane_blocks]

        for h in range(pairs):
            hl = slice(h * LANES, (h + 1) * LANES)
            z = _dot_nt(qs_ref[h], k_ref[0, pl.ds(start, tk), hl])
            lbs, zls, tots = [], [], []
            for r in strips:
                zc = [z[r:r + STRIP, c:c + LANES] for c in lane_blocks]
                zl = [jnp.minimum(x, 0.0) - jnp.log2(1.0 + jnp.exp2(jnp.minimum(x, -x))) for x in zc]
                ls = [a - x for a, x in zip(zl, zc)]
                if masked:
                    ls = [jnp.where(v, l, 0.0) for v, l in zip(visible(r), ls)]
                zls.append(zl)
                lbs.append(jnp.concatenate([l.astype(_bf16) for l in ls], axis=1))
                tots.append(jnp.sum(functools.reduce(jnp.add, ls), axis=1, keepdims=True))
            cum = jnp.dot(jnp.concatenate(lbs, axis=0), upper_ref[...], preferred_element_type=_f32)
            ws = []
            for i, r in enumerate(strips):
                carry = carry_ref[h, r:r + STRIP, :]
                w = [jnp.exp2(zl + cum[r:r + STRIP, c:c + LANES] + carry) for zl, c in zip(zls[i], lane_blocks)]
                if masked:
                    w = [jnp.where(v, x, 0.0) for v, x in zip(visible(r), w)]
                carry_ref[h, r:r + STRIP, :] = carry + tots[i]
                ws.append(jnp.concatenate([x.astype(_bf16) for x in w], axis=1))
            acc_ref[h] += jnp.dot(jnp.concatenate(ws, axis=0), v_ref[0, pl.ds(start, tk), hl],
                                  preferred_element_type=_f32)

    @pl.loop(0, kb_hi - kb_full)
    def _(i):
        tile(kb_hi - i, True)

    def alive(st):
        kb, least_dead = st
        return jnp.logical_and(kb >= 0, least_dead > STICK_DEAD)

    def walk(st):
        kb, _ = st
        tile(kb, False)
        return kb - 1, jnp.max(carry_ref[...])

    lax.while_loop(alive, walk, (kb_full, jnp.max(carry_ref[...])))

    lane = lax.broadcasted_iota(jnp.int32, (tq, LANES), 1)
    for h in range(pairs):
        o_ref[0, :, h * LANES:(h + 1) * LANES] = jnp.where(
            lane < HEAD_DIM, acc_ref[h, 0:tq, :], acc_ref[h, tq:2 * tq, :]).astype(o_ref.dtype)


def _sb_call(q, k, v, *, pos0, tq, tk):
    b, t, width = q.shape
    tkv = k.shape[1]
    pairs = width // LANES
    qspec = pl.BlockSpec((1, tq, width), lambda bi, qi: (bi, qi, 0))
    kvspec = pl.BlockSpec((1, tkv, width), lambda bi, qi: (bi, 0, 0), pipeline_mode=pl.Buffered(1))
    return pl.pallas_call(
        functools.partial(_sb_kernel, pos0=pos0, tq=tq, tk=tk),
        grid=(b, t // tq),
        in_specs=[qspec, kvspec, kvspec],
        out_specs=qspec,
        out_shape=jax.ShapeDtypeStruct((b, t, width), _bf16),
        scratch_shapes=[pltpu.VMEM((pairs, 2 * tq, LANES), _bf16),
                        pltpu.VMEM((tk, tk), _bf16),
                        pltpu.VMEM((pairs, 2 * tq, LANES), _f32),
                        pltpu.VMEM((pairs, 2 * tq, LANES), _f32)],
        compiler_params=pltpu.CompilerParams(
            dimension_semantics=("arbitrary", "arbitrary"),
            vmem_limit_bytes=VMEM_LIMIT_BYTES),
        name="stick_breaking_attention",
    )(q, k, v)


def _diff_kernel(q_ref, k_ref, v_ref, lq1_ref, lk1_ref, lq2_ref, lk2_ref, g_ref, o_ref,
                 qs_ref, m_ref, acc_ref, *, pos0, tq, tk, lambda_init):
    heads = q_ref.shape[2] // LANES
    p0 = pos0 + pl.program_id(1) * tq
    vis_end_first = (p0 // CHUNK + 1) * CHUNK
    vis_end_last = ((p0 + tq - 1) // CHUNK + 1) * CHUNK
    kb_hi = (vis_end_last - 1) // tk
    kb_full = vis_end_first // tk - 1
    for h in range(heads):
        qs_ref[h] = _stack_halves(q_ref[0, :, h * LANES:(h + 1) * LANES])
    m_ref[...] = jnp.full(m_ref.shape, NEG_BIG, _f32)
    acc_ref[...] = jnp.zeros(acc_ref.shape, _f32)
    col = lax.broadcasted_iota(jnp.int32, (STRIP, LANES), 1)
    ones = jnp.ones((tk, LANES), _bf16)

    def tile(kb, masked):
        start = pl.multiple_of(kb * tk, tk)
        for h in range(heads):
            hl = slice(h * LANES, (h + 1) * LANES)
            s = _dot_nt(qs_ref[h], k_ref[0, pl.ds(start, tk), hl])
            probs = []
            for r in range(0, 2 * tq, STRIP):
                rows = slice(r, r + STRIP)
                cols = [s[rows, c:c + LANES] for c in range(0, tk, LANES)]
                if masked:
                    limit = ((r % tq) // CHUNK + 1) * CHUNK + p0 - start
                    vis = [col + c < limit for c in range(0, tk, LANES)]
                    cols = [jnp.where(v, x, NEG_BIG) for v, x in zip(vis, cols)]
                m_old = m_ref[h, rows, :]
                m_new = jnp.maximum(m_old, jnp.max(functools.reduce(jnp.maximum, cols), axis=1, keepdims=True))
                ps = [jnp.exp2(x - m_new) for x in cols]
                if masked:
                    ps = [jnp.where(v, x, 0.0) for v, x in zip(vis, ps)]
                alpha = jnp.exp2(m_old - m_new)
                m_ref[h, rows, :] = m_new
                acc_ref[h, rows, 0:LANES] = alpha * acc_ref[h, rows, 0:LANES]
                acc_ref[h, rows, LANES:2 * LANES] = alpha * acc_ref[h, rows, LANES:2 * LANES]
                probs.append(jnp.concatenate([x.astype(_bf16) for x in ps], axis=1))
            p = jnp.concatenate(probs, axis=0)
            v1 = jnp.concatenate([v_ref[0, pl.ds(start, tk), hl], ones], axis=1)
            acc_ref[h] += jnp.dot(p, v1, preferred_element_type=_f32)

    @pl.loop(0, kb_hi - kb_full)
    def _(i):
        tile(kb_hi - i, True)

    @pl.loop(0, kb_full + 1)
    def _(i):
        tile(kb_full - i, False)

    lam = (jnp.exp(jnp.sum(lq1_ref[...] * lk1_ref[...], axis=1, keepdims=True))
           - jnp.exp(jnp.sum(lq2_ref[...] * lk2_ref[...], axis=1, keepdims=True)) + lambda_init)
    for h in range(heads):
        o = (acc_ref[h, 0:tq, 0:LANES] / acc_ref[h, 0:tq, LANES:2 * LANES]
             - lam * (acc_ref[h, tq:2 * tq, 0:LANES] / acc_ref[h, tq:2 * tq, LANES:2 * LANES]))
        o = o * lax.rsqrt(jnp.mean(o * o, axis=1, keepdims=True) + LN_EPS)
        o_ref[0, :, h * LANES:(h + 1) * LANES] = (o * g_ref[...] * (1.0 - lambda_init)).astype(o_ref.dtype)


def _diff_call(q, k, v, lq1, lk1, lq2, lk2, subln_g, *, pos0, tq, tk, lambda_init):
    b, t, width = q.shape
    tkv = k.shape[1]
    heads = width // LANES
    assert tq % CHUNK == 0 and pos0 % CHUNK == 0 and CHUNK % STRIP == 0
    qspec = pl.BlockSpec((1, tq, width), lambda bi, qi: (bi, qi, 0))
    kvspec = pl.BlockSpec((1, tkv, width), lambda bi, qi: (bi, 0, 0), pipeline_mode=pl.Buffered(1))
    small = (lq1, lk1, lq2, lk2, subln_g)
    return pl.pallas_call(
        functools.partial(_diff_kernel, pos0=pos0, tq=tq, tk=tk, lambda_init=lambda_init),
        grid=(b, t // tq),
        in_specs=[qspec, kvspec, kvspec] + [_const_spec(s.shape) for s in small],
        out_specs=qspec,
        out_shape=jax.ShapeDtypeStruct((b, t, width), _bf16),
        scratch_shapes=[pltpu.VMEM((heads, 2 * tq, LANES), _bf16),
                        pltpu.VMEM((heads, 2 * tq, LANES), _f32),
                        pltpu.VMEM((heads, 2 * tq, 2 * LANES), _f32)],
        compiler_params=pltpu.CompilerParams(
            dimension_semantics=("arbitrary", "arbitrary"),
            vmem_limit_bytes=VMEM_LIMIT_BYTES),
        name="differential_attention",
    )(q, k, v, *small)


def _rope_tables(pos, reps):
    half = HEAD_DIM // 2
    inv_freq = ROPE_THETA ** (-jnp.arange(half, dtype=_f32) / half)
    ang = pos.astype(_f32)[:, None] * inv_freq[None, :]
    cos, sin = jnp.cos(ang), jnp.sin(ang)
    cos_t = jnp.tile(jnp.concatenate([cos, cos], axis=1), (reps, LANES // HEAD_DIM))
    sin_t = jnp.tile(jnp.concatenate([-sin, sin], axis=1), (reps, LANES // HEAD_DIM))
    return cos_t, sin_t


def _layer(x, caches, pos0, p, lambda_init, alpha, *, tm, tq, tq_diff, tk):
    b, t, d = x.shape
    n = b * t
    width = p["w_in"].shape[1] // 6
    tm = min(tm, n)
    if t >= tm:
        cos_t, sin_t = _rope_tables(pos0 + jnp.arange(t), 1)
    else:
        cos_t, sin_t = _rope_tables(pos0 + jnp.arange(t), tm // t)
    h, sbk, sbv, dfk, dfv, sbq, dfq, sbkb, sbvb, dfkb, dfvb = _pre_call(
        x.reshape(n, d), cos_t, sin_t, p["ln1_g"], p["ln1_b"],
        p["ffn1_wg"], p["ffn1_wu"], p["ffn1_wd"], p["w_in"], alpha=alpha, tm=tm // 2)
    rows = (sbk, sbv, dfk, dfv)
    r3 = lambda a: a.reshape(b, t, width)
    if caches is None:
        k_sb, v_sb, k_df, v_df = r3(sbkb), r3(sbvb), r3(dfkb), r3(dfvb)
    else:
        tkv = caches[0].shape[1] + t
        pad = (-tkv) % tk
        cat = lambda c, a: jnp.concatenate(
            [c.reshape(b, -1, width).astype(a.dtype), r3(a), jnp.zeros((b, pad, width), a.dtype)], axis=1)
        k_sb, v_sb, k_df, v_df = (cat(c, a) for c, a in zip(caches, (sbkb, sbvb, dfkb, dfvb)))
    o_sb = _sb_call(r3(sbq), k_sb, v_sb, pos0=pos0, tq=tq, tk=tk)
    o_df = _diff_call(r3(dfq), k_df, v_df, p["lambda_q1"], p["lambda_k1"], p["lambda_q2"],
                      p["lambda_k2"], p["subln_g"], pos0=pos0, tq=tq_diff, tk=tk, lambda_init=lambda_init)
    y = _post_call(o_sb.reshape(n, width), o_df.reshape(n, width), h, p["wo_sb"], p["wo_df"],
                   p["ln2_g"], p["ln2_b"], p["ffn2_wg"], p["ffn2_wu"], p["ffn2_wd"],
                   p["ln3_g"], p["ln3_b"], alpha=alpha, tm=tm)
    return y.reshape(b, t, d), rows


def _run(x_prompt, x_sample, caches, params, *, depth, tm, tq_prompt, tq_prompt_diff, tk):
    assert depth == 1, "the projection / attention layouts assume a single layer"
    alpha = (2 * depth) ** 0.25
    lambda_init = 0.8 - 0.6 * math.exp(-0.3 * 0)
    width = params["w_in"].shape[-1] // 6
    p = {}
    for name, val in params.items():
        val = val[0]
        if name.startswith("ffn") or name == "w_in":
            p[name] = val.astype(_bf16)
        elif name == "w_o":
            p["wo_sb"] = val[:width].astype(_bf16)
            p["wo_df"] = val[width:].astype(_bf16)
        else:
            p[name] = val.reshape(1, -1)
    yp, rp = _layer(x_prompt, None, 0, p, lambda_init, alpha, tm=tm, tq=tq_prompt,
                    tq_diff=min(tq_prompt_diff, x_prompt.shape[1]), tk=tk)
    past = caches[0].shape[2]
    cs = tuple(c[0] for c in caches)
    ys, rs = _layer(x_sample, cs, past, p, lambda_init, alpha, tm=tm, tq=x_sample.shape[1],
                    tq_diff=x_sample.shape[1], tk=tk)
    return yp, ys, rp, rs


def kernel(x_prompt, x_sample, cache_sb_k, cache_sb_v, cache_diff_k, cache_diff_v, ln1_g, ln1_b, ffn1_wg, ffn1_wu, ffn1_wd, w_in, lambda_q1, lambda_k1, lambda_q2, lambda_k2, subln_g, w_o, ln2_g, ln2_b, ffn2_wg, ffn2_wu, ffn2_wd, ln3_g, ln3_b):
    params = dict(ln1_g=ln1_g, ln1_b=ln1_b, ffn1_wg=ffn1_wg, ffn1_wu=ffn1_wu, ffn1_wd=ffn1_wd,
                  w_in=w_in, lambda_q1=lambda_q1, lambda_k1=lambda_k1, lambda_q2=lambda_q2,
                  lambda_k2=lambda_k2, subln_g=subln_g, w_o=w_o, ln2_g=ln2_g, ln2_b=ln2_b,
                  ffn2_wg=ffn2_wg, ffn2_wu=ffn2_wu, ffn2_wd=ffn2_wd, ln3_g=ln3_g, ln3_b=ln3_b)
    depth = w_in.shape[0]
    yp, ys, rp, rs = _run(x_prompt, x_sample, (cache_sb_k, cache_sb_v, cache_diff_k, cache_diff_v),
                          params, depth=depth, tm=512, tq_prompt=256, tq_prompt_diff=512, tk=256)
    bp, tp = x_prompt.shape[:2]
    bs, ts = x_sample.shape[:2]
    heads = w_in.shape[-1] // 6 // HEAD_DIM

    def shapes(rows, b, t):
        sbk, sbv, dfk, dfv = rows
        return (sbk.reshape(depth, b, t, heads, HEAD_DIM), sbv.reshape(depth, b, t, heads, HEAD_DIM),
                dfk.reshape(depth, b, t, heads // 2, 2, HEAD_DIM),
                dfv.reshape(depth, b, t, heads // 2, 2 * HEAD_DIM))

    return (yp, ys) + shapes(rp, bp, tp) + shapes(rs, bs, ts)
```
